```python
import jax, jax.numpy as jnp
from jax import lax
import numpy as np

D_MODEL = 1024
BATCH = 16
SEQ = 2048
DEPTH = 2

CHUNK = 64
MIX_WIDTH = D_MODEL
RWKV_HEADS = 8
RWKV_HEAD_DIM = 64
RWKV_WIDTH = RWKV_HEADS * RWKV_HEAD_DIM
DECAY_LORA = 64
ICL_LORA = 64
GATE_LORA = 128
RWKV_IN = 3 * RWKV_WIDTH + DECAY_LORA + ICL_LORA + GATE_LORA
RET_HEADS = 4
RET_QK_DIM = 64
RET_V_DIM = 128
RET_QK_WIDTH = RET_HEADS * RET_QK_DIM
RET_V_WIDTH = RET_HEADS * RET_V_DIM
RET_IN = 2 * RET_QK_WIDTH + 2 * RET_V_WIDTH
PROJ_WIDTH = RWKV_IN + RET_IN
D_FF = 2816
ROPE_BASE = 10000.0
NORM_EPS = 1e-6
LN_X_EPS = 64e-5

kernel_name = "hybrid_rwkv7_retnet_macaron_trunk"


def rms_norm(x, gain, eps=NORM_EPS):
    xf = x.astype(jnp.float32)
    y = xf * lax.rsqrt(jnp.mean(xf * xf, axis=-1, keepdims=True) + eps)
    return (y * gain.astype(jnp.float32)).astype(x.dtype)


def swiglu(h, w_gate, w_up, w_down):
    return (jax.nn.silu(h @ w_gate) * (h @ w_up)) @ w_down


def token_shift(p, mu):
    prev = jnp.pad(p, ((0, 0), (1, 0), (0, 0)))[:, :-1]
    return p + mu * (prev - p)


def rwkv7_scan(r, w, k, v, a, b):
    B, S, H, N = r.shape

    def step(state, inp):
        r_t, w_t, k_t, v_t, a_t, b_t = inp
        sa = jnp.einsum('bhvk,bhk->bhv', state, a_t)
        state = (state * w_t[:, :, None, :] + sa[..., None] * b_t[:, :, None, :]
                 + v_t[..., None] * k_t[:, :, None, :])
        return state, jnp.einsum('bhvk,bhk->bhv', state, r_t)

    xs = (jnp.moveaxis(r, 1, 0), jnp.moveaxis(w, 1, 0), jnp.moveaxis(k, 1, 0),
          jnp.moveaxis(v, 1, 0), jnp.moveaxis(a, 1, 0), jnp.moveaxis(b, 1, 0))
    state0 = jnp.zeros((B, H, N, N), jnp.float32)
    _, y = lax.scan(step, state0, xs)
    return jnp.moveaxis(y, 0, 1)


def rwkv7_group(p, mu, w0, w_lora_up, a0, a_lora_up, g_lora_up, k_k, k_a, r_k,
                ln_w, ln_b):
    B, S, _ = p.shape
    p = token_shift(p, mu)
    o1 = RWKV_WIDTH
    o2 = 2 * RWKV_WIDTH
    o3 = 3 * RWKV_WIDTH
    o4 = o3 + DECAY_LORA
    o5 = o4 + ICL_LORA
    r, k, v, w_d, a_d, g_d = jnp.split(p, [o1, o2, o3, o4, o5], axis=-1)
    f32 = jnp.float32
    log_w = -jax.nn.softplus(-(w0 + jnp.tanh(w_d) @ w_lora_up).astype(f32)) - 0.5
    decay = jnp.exp(-jnp.exp(log_w))
    a = jax.nn.sigmoid((a0 + a_d @ a_lora_up).astype(f32))
    g = (jax.nn.sigmoid(g_d) @ g_lora_up).astype(f32)

    def heads(t):
        return t.astype(f32).reshape(B, S, RWKV_HEADS, RWKV_HEAD_DIM)

    kk = heads(k * k_k)
    kk = kk * lax.rsqrt(jnp.maximum(jnp.sum(kk * kk, axis=-1, keepdims=True), 1e-24))
    k_h = heads(k.astype(f32) * (1.0 + (a - 1.0) * k_a.astype(f32)))
    r_h = heads(r)
    v_h = heads(v)
    a_h = heads(a)
    y = rwkv7_scan(r_h, heads(decay), k_h, v_h, -kk, kk * a_h)
    mean = jnp.mean(y, axis=-1, keepdims=True)
    var = jnp.mean(jnp.square(y - mean), axis=-1, keepdims=True)
    y = ((y - mean) * lax.rsqrt(var + LN_X_EPS)
         * ln_w.astype(f32).reshape(RWKV_HEADS, RWKV_HEAD_DIM)
         + ln_b.astype(f32).reshape(RWKV_HEADS, RWKV_HEAD_DIM))
    y = y + jnp.sum(r_h * k_h * r_k.astype(f32), axis=-1, keepdims=True) * v_h
    return (y.reshape(B, S, RWKV_WIDTH) * g).astype(p.dtype)


def rotary(x, pos):
    half = x.shape[-1] // 2
    inv_freq = 1.0 / (ROPE_BASE ** jnp.linspace(0.0, 1.0, half, dtype=jnp.float32))
    ang = pos[:, None] * inv_freq[None, :]
    cos = jnp.cos(ang)[None, :, None, :]
    sin = jnp.sin(ang)[None, :, None, :]
    x1, x2 = x[..., :half], x[..., half:]
    return jnp.concatenate([x1 * cos - x2 * sin, x1 * sin + x2 * cos], axis=-1)


def retention_chunkwise(q, k, v):
    B, S, H, dk = q.shape
    dv = v.shape[-1]
    nc = S // CHUNK
    log_gamma = jnp.log(1.0 - jnp.power(2.0, -5.0 - jnp.arange(H, dtype=jnp.float32)))
    qc = q.reshape(B, nc, CHUNK, H, dk)
    kc = k.reshape(B, nc, CHUNK, H, dk)
    vc = v.reshape(B, nc, CHUNK, H, dv)
    pos = jnp.arange(CHUNK, dtype=jnp.float32)
    dist = jnp.abs(pos[:, None] - pos[None, :])
    intra_decay = jnp.exp(log_gamma[:, None, None] * dist)
    scores = jnp.einsum('bnihd,bnjhd->bnhij', qc, kc) * intra_decay
    intra = jnp.einsum('bnhij,bnjhe->bnihe', scores, vc)
    key_w = jnp.exp(log_gamma[:, None] * (CHUNK - 1.0 - pos)[None, :])
    chunk_kv = jnp.einsum('bnjhd,hj,bnjhe->bnhde', kc, key_w, vc)
    chunk_decay = jnp.exp(log_gamma * CHUNK)[None, :, None, None]

    def step(state, kv_n):
        return state * chunk_decay + kv_n, state

    _, prev = lax.scan(step, jnp.zeros((B, H, dk, dv), jnp.float32),
                       jnp.moveaxis(chunk_kv, 1, 0))
    prev = jnp.moveaxis(prev, 0, 1)
    query_w = jnp.exp(log_gamma[:, None] * (pos + 1.0)[None, :])
    cross = jnp.einsum('bnihd,hi,bnhde->bnihe', qc, query_w, prev)
    return (intra + cross).reshape(B, S, H, dv)


def retnet_group(p):
    B, S, _ = p.shape
    f32 = jnp.float32
    q, k, v, g = jnp.split(p, [RET_QK_WIDTH, 2 * RET_QK_WIDTH,
                               2 * RET_QK_WIDTH + RET_V_WIDTH], axis=-1)
    pos = jnp.arange(S, dtype=f32)
    q = rotary(q.astype(f32).reshape(B, S, RET_HEADS, RET_QK_DIM), pos)
    k = rotary(k.astype(f32).reshape(B, S, RET_HEADS, RET_QK_DIM), pos) * (RET_QK_DIM ** -0.5)
    v = v.astype(f32).reshape(B, S, RET_HEADS, RET_V_DIM)
    o = retention_chunkwise(q, k, v)
    o = o * lax.rsqrt(jnp.mean(o * o, axis=-1, keepdims=True) + NORM_EPS)
    return (o.reshape(B, S, RET_V_WIDTH) * jax.nn.silu(g.astype(f32))).astype(p.dtype)


def setup_inputs(seed: int = 0) -> dict:
    key = jax.random.key(seed)
    ks = jax.random.split(key, 32)
    f32 = jnp.float32

    def nrm(k, shape, scale):
        return jax.random.normal(k, shape, f32) * scale

    decay_speed = -7.0 + 5.0 * jnp.linspace(0.0, 1.0, RWKV_WIDTH, dtype=f32) ** 0.85 + 0.5
    return {
        "x": nrm(ks[0], (BATCH, SEQ, D_MODEL), 1.0),
        "ffn1_norm": 1.0 + nrm(ks[1], (DEPTH, D_MODEL), 0.1),
        "ffn1_w_gate": nrm(ks[2], (DEPTH, D_MODEL, D_FF), D_MODEL ** -0.5),
        "ffn1_w_up": nrm(ks[3], (DEPTH, D_MODEL, D_FF), D_MODEL ** -0.5),
        "ffn1_w_down": nrm(ks[4], (DEPTH, D_FF, D_MODEL), D_FF ** -0.5),
        "mix_norm": 1.0 + nrm(ks[5], (DEPTH, D_MODEL), 0.1),
        "w_in": nrm(ks[6], (DEPTH, D_MODEL, PROJ_WIDTH), D_MODEL ** -0.5),
        "shift_mu": jax.random.uniform(ks[7], (DEPTH, RWKV_IN), f32),
        "w0": decay_speed[None, :] + nrm(ks[8], (DEPTH, RWKV_WIDTH), 0.1),
        "w_lora_up": nrm(ks[9], (DEPTH, DECAY_LORA, RWKV_WIDTH), 0.1),
        "a0": nrm(ks[10], (DEPTH, RWKV_WIDTH), 0.1),
        "a_lora_up": nrm(ks[11], (DEPTH, ICL_LORA, RWKV_WIDTH), 0.5 * ICL_LORA ** -0.5),
        "g_lora_up": nrm(ks[12], (DEPTH, GATE_LORA, RWKV_WIDTH), GATE_LORA ** -0.5),
        "k_k": 0.85 + nrm(ks[13], (DEPTH, RWKV_WIDTH), 0.05),
        "k_a": 1.0 + nrm(ks[14], (DEPTH, RWKV_WIDTH), 0.05),
        "r_k": nrm(ks[15], (DEPTH, RWKV_HEADS, RWKV_HEAD_DIM), 0.1),
        "ln_x_w": 1.0 + nrm(ks[16], (DEPTH, RWKV_WIDTH), 0.1),
        "ln_x_b": nrm(ks[17], (DEPTH, RWKV_WIDTH), 0.01),
        "w_out": nrm(ks[18], (DEPTH, MIX_WIDTH, D_MODEL), MIX_WIDTH ** -0.5),
        "ffn2_norm": 1.0 + nrm(ks[19], (DEPTH, D_MODEL), 0.1),
        "ffn2_w_gate": nrm(ks[20], (DEPTH, D_MODEL, D_FF), D_MODEL ** -0.5),
        "ffn2_w_up": nrm(ks[21], (DEPTH, D_MODEL, D_FF), D_MODEL ** -0.5),
        "ffn2_w_down": nrm(ks[22], (DEPTH, D_FF, D_MODEL), D_FF ** -0.5),
        "final_norm": 1.0 + nrm(ks[23], (D_MODEL,), 0.1),
    }


def reference(x, ffn1_norm, ffn1_w_gate, ffn1_w_up, ffn1_w_down, mix_norm, w_in,
              shift_mu, w0, w_lora_up, a0, a_lora_up, g_lora_up, k_k, k_a, r_k,
              ln_x_w, ln_x_b, w_out, ffn2_norm, ffn2_w_gate, ffn2_w_up, ffn2_w_down,
              final_norm):
    for l in range(DEPTH):
        x = x + 0.5 * swiglu(rms_norm(x, ffn1_norm[l]), ffn1_w_gate[l], ffn1_w_up[l],
                             ffn1_w_down[l])
        h = rms_norm(x, mix_norm[l])
        proj = h @ w_in[l]
        y_rwkv = rwkv7_group(proj[..., :RWKV_IN], shift_mu[l], w0[l], w_lora_up[l], a0[l],
                             a_lora_up[l], g_lora_up[l], k_k[l], k_a[l], r_k[l],
                             ln_x_w[l], ln_x_b[l])
        y_ret = retnet_group(proj[..., RWKV_IN:])
        mixed = jnp.concatenate([y_rwkv, y_ret], axis=-1).astype(x.dtype)
        x = x + mixed @ w_out[l]
        x = x + 0.5 * swiglu(rms_norm(x, ffn2_norm[l]), ffn2_w_gate[l], ffn2_w_up[l],
                             ffn2_w_down[l])
    return rms_norm(x, final_norm)
```

```python
import functools

import jax
import jax.numpy as jnp
from jax import lax
from jax.experimental import pallas as pl
from jax.experimental.pallas import tpu as pltpu

D_MODEL = 1024
D_FF = 2816
CHUNK = 64
RWKV_HEADS = 8
RWKV_HEAD_DIM = 64
RWKV_WIDTH = RWKV_HEADS * RWKV_HEAD_DIM
DECAY_LORA = 64
ICL_LORA = 64
GATE_LORA = 128
RWKV_IN = 3 * RWKV_WIDTH + DECAY_LORA + ICL_LORA + GATE_LORA
RET_HEADS = 4
RET_QK_DIM = 64
RET_V_DIM = 128
RET_QK_WIDTH = RET_HEADS * RET_QK_DIM
RET_V_WIDTH = RET_HEADS * RET_V_DIM
RET_IN = 2 * RET_QK_WIDTH + 2 * RET_V_WIDTH
PROJ_WIDTH = RWKV_IN + RET_IN
ROPE_BASE = 10000.0
NORM_EPS = 1e-6
LN_X_EPS = 64e-5

GROUP = 4
QUAD = GROUP * RWKV_HEAD_DIM
FFN_ROWS = 512
MIX_ROWS = 256
VMEM_LIMIT_BYTES = 58 * 1024 * 1024

F32 = jnp.float32
BF16 = jnp.bfloat16
NN = (((1,), (0,)), ((), ()))
NT = (((1,), (1,)), ((), ()))
TN = (((0,), (0,)), ((), ()))


def _dot(a, b, dims=NN):
    return lax.dot_general(a, b, dims, preferred_element_type=F32)


def _split(x):
    hi = x.astype(BF16)
    lo = (x - hi.astype(F32)).astype(BF16)
    return hi, lo


def _mm(a, b, dims=NN, passes=1):
    if passes == 1:
        return _dot(a.astype(BF16), b.astype(BF16), dims)
    ah, al = _split(a)
    bh, bl = _split(b)
    return _dot(ah, bh, dims) + (_dot(ah, bl, dims) + _dot(al, bh, dims))


def _const_mm(c_bf16, x, terms):
    acc = None
    rem = x
    for _ in range(terms):
        piece = rem.astype(BF16)
        rem = rem - piece.astype(F32)
        d = _dot(c_bf16, piece)
        acc = d if acc is None else acc + d
    return acc


def _x_const_mm(x, c_bf16, terms):
    acc = None
    rem = x
    for _ in range(terms):
        piece = rem.astype(BF16)
        rem = rem - piece.astype(F32)
        d = _dot(piece, c_bf16)
        acc = d if acc is None else acc + d
    return acc


def _sigmoid(x):
    return 1.0 / (1.0 + jnp.exp(-x))


def _softplus(x):
    return jnp.maximum(x, 0.0) + jnp.log(1.0 + jnp.exp(-jnp.abs(x)))


def _bd_rows(x, group_lanes):
    lane_head = lax.broadcasted_iota(jnp.int32, x.shape, 1) // group_lanes
    return jnp.concatenate(
        [jnp.where(lane_head == h, x, 0.0) for h in range(GROUP)], axis=0)


def _mask_bd(z, group_lanes):
    lane_head = lax.broadcasted_iota(jnp.int32, (CHUNK, z.shape[1]), 1) // group_lanes
    return jnp.concatenate(
        [jnp.where(lane_head == h, z[h * CHUNK:(h + 1) * CHUNK], 0.0) for h in range(GROUP)],
        axis=0)


def _tri_inverse(n, passes):
    row = lax.broadcasted_iota(jnp.int32, n.shape, 0)
    col = lax.broadcasted_iota(jnp.int32, n.shape, 1) % CHUNK
    same16 = (row // 16) == (col // 16)
    same32 = (row // 32) == (col // 32)
    n0 = jnp.where(same16, n, 0.0)
    n1 = jnp.where(same32, jnp.where(same16, 0.0, n), 0.0)
    n2 = jnp.where(same32, 0.0, n)

    def prod(a, b):
        return _mm(a, _bd_rows(b, CHUNK), NN, passes)

    t = jnp.where(row == col, 1.0, 0.0) + n0
    m = prod(n0, n0)
    t = t + prod(t, m)
    m = prod(m, m)
    t = t + prod(t, m)
    m = prod(m, m)
    t = t + prod(t, m)
    t = t + prod(prod(t, n1), t)
    t = t + prod(prod(t, n2), t)
    return t


def _rwkv_chunk(at, rt, bt, kt, bw, kw, v, s, w_end, inv_passes):
    lhs = jnp.concatenate([at, rt], axis=0)
    rhs = jnp.concatenate([_bd_rows(bt, CHUNK), _bd_rows(kt, CHUNK)], axis=0)
    a = _mm(lhs, rhs, NT)
    row = lax.broadcasted_iota(jnp.int32, at.shape, 0)
    col = lax.broadcasted_iota(jnp.int32, at.shape, 1) % CHUNK
    strict = col < row
    incl = col <= row
    n = jnp.where(strict, a[:CHUNK, :QUAD], 0.0)
    a_ak = jnp.where(strict, a[:CHUNK, QUAD:], 0.0)
    a_rb = jnp.where(incl, a[CHUNK:, :QUAD], 0.0)
    a_rk = jnp.where(incl, a[CHUNK:, QUAD:], 0.0)
    t = _tri_inverse(n, inv_passes)
    bd_v = _bd_rows(v, CHUNK)
    a_hat = _mm(t, _bd_rows(at, CHUNK))
    u_loc = _mm(t, _bd_rows(_mm(a_ak, bd_v), CHUNK))
    q_hat = rt + _mm(a_rb, _bd_rows(a_hat, CHUNK))
    y_loc = _mm(jnp.concatenate([a_rb, a_rk], axis=1),
                jnp.concatenate([_bd_rows(u_loc, CHUNK), bd_v], axis=0))
    y = _mm(q_hat, s, NT) + y_loc
    m_c = _mask_bd(_mm(bw, a_hat, TN), CHUNK)
    s_loc = _mask_bd(_mm(jnp.concatenate([u_loc, v], axis=0),
                         jnp.concatenate([bw, kw], axis=0), TN), CHUNK)
    s_new = s * w_end + _mm(s, m_c, NT) + s_loc
    return y, s_new


def _ret_chunk(q, q_w, k, k_w, v, s, intra_decay, chunk_decay):
    scores = _mm(q, _bd_rows(k, RET_QK_DIM), NT) * intra_decay
    o = _mm(scores, _bd_rows(v, RET_V_DIM)) + _mm(q_w, s)
    kv = _mask_bd(_mm(k_w, v, TN), RET_V_DIM)
    return o, s * chunk_decay + kv


def _rope(x, cos, sin_signed):
    half = RET_QK_DIM // 2
    width = x.shape[1]
    lane = lax.broadcasted_iota(jnp.int32, x.shape, 1) % RET_QK_DIM
    partner = jnp.where(lane < half, pltpu.roll(x, width - half, 1), pltpu.roll(x, half, 1))
    return x * cos + partner * sin_signed


def _mixer_kernel(x_ref, gain_ref, win_ref, mu_ref, lora_ref, glora_ref, vecs_ref, wout_ref,
                  tri_ref, ones_ref, hones_ref, cos_ref, sin_ref, dd_ref, qw_ref, kw_ref, g64_ref,
                  o_ref,
                  carry_ref, srw_ref, sret_ref,
                  at_s, rt_s, bt_s, kt_s, bw_s, kws_s, v_s, wend_s, y_s,
                  rq_s, rqw_s, rk_s, rkw_s, rv_s, ro_s, *, rows, inv_passes):
    @pl.when(pl.program_id(1) == 0)
    def _():
        carry_ref[...] = jnp.zeros_like(carry_ref)
        srw_ref[...] = jnp.zeros_like(srw_ref)
        sret_ref[...] = jnp.zeros_like(sret_ref)

    x = x_ref[0]
    h = x * lax.rsqrt(jnp.mean(x * x, axis=-1, keepdims=True) + NORM_EPS) * gain_ref[...]
    p = _dot(h.astype(BF16), win_ref[...])

    pr = p[:, :RWKV_IN]
    first = lax.broadcasted_iota(jnp.int32, pr.shape, 0) == 0
    prev = jnp.where(first, carry_ref[...], pltpu.roll(pr, 1, 0))
    carry_ref[...] = pr[rows - 1:rows, :]
    ps = pr + mu_ref[...] * (prev - pr)
    w = RWKV_WIDTH
    r, k, v = ps[:, :w], ps[:, w:2 * w], ps[:, 2 * w:3 * w]
    wa = ps[:, 3 * w:3 * w + DECAY_LORA + ICL_LORA]
    gd = ps[:, 3 * w + DECAY_LORA + ICL_LORA:]
    is_decay = lax.broadcasted_iota(jnp.int32, wa.shape, 1) < DECAY_LORA
    lora = _dot(jnp.where(is_decay, jnp.tanh(wa), wa).astype(BF16), lora_ref[...])
    vecs = vecs_ref[...]
    w0, a0, k_k, k_a, r_k, ln_w, ln_b = (vecs[i:i + 1] for i in range(7))
    log_w = -_softplus(-(w0 + lora[:, :w])) - 0.5
    log_decay = -jnp.exp(log_w)
    a = _sigmoid(a0 + lora[:, w:])
    g = _dot(_sigmoid(gd).astype(BF16), glora_ref[...])
    hones = hones_ref[...]
    kk = k * k_k
    kk = kk * lax.rsqrt(jnp.maximum(_x_const_mm(kk * kk, hones, 2), 1e-24))
    k_mod = k * (1.0 + (a - 1.0) * k_a)
    bonus = _x_const_mm(r * k_mod * r_k, hones, 2) * v
    cum = _const_mm(tri_ref[...], log_decay, 3)
    tot = _const_mm(ones_ref[...], log_decay, 3)
    w_inv = jnp.exp(-cum)
    w_end = jnp.exp(tot - cum)
    at_s[...] = -kk * jnp.exp(cum - log_decay)
    rt_s[...] = r * jnp.exp(cum)
    bt_s[...] = kk * a * w_inv
    kt_s[...] = k_mod * w_inv
    bw_s[...] = kk * a * w_end
    kws_s[...] = k_mod * w_end
    v_s[...] = v
    wend_s[...] = jnp.exp(tot)

    q0 = RWKV_IN
    cos = cos_ref[...]
    sin = sin_ref[...]
    rq = _rope(p[:, q0:q0 + RET_QK_WIDTH], cos, sin)
    rk = _rope(p[:, q0 + RET_QK_WIDTH:q0 + 2 * RET_QK_WIDTH], cos, sin) * (RET_QK_DIM ** -0.5)
    rq_s[...] = rq
    rqw_s[...] = rq * qw_ref[...]
    rk_s[...] = rk
    rkw_s[...] = rk * kw_ref[...]
    rv_s[...] = p[:, q0 + 2 * RET_QK_WIDTH:q0 + 2 * RET_QK_WIDTH + RET_V_WIDTH]
    rg = p[:, q0 + 2 * RET_QK_WIDTH + RET_V_WIDTH:]

    intra_decay = dd_ref[...]
    chunk_decay = g64_ref[...]

    def chunk_body(c, carry):
        rs = pl.ds(pl.multiple_of(c * CHUNK, CHUNK), CHUNK)
        for qd in range(RWKV_HEADS // GROUP):
            ls = slice(qd * QUAD, (qd + 1) * QUAD)
            y, s_new = _rwkv_chunk(at_s[rs, ls], rt_s[rs, ls], bt_s[rs, ls], kt_s[rs, ls],
                                   bw_s[rs, ls], kws_s[rs, ls], v_s[rs, ls], srw_ref[qd],
                                   wend_s[pl.ds(c * CHUNK, 1), ls], inv_passes)
            y_s[rs, ls] = y
            srw_ref[qd] = s_new
        o, sr_new = _ret_chunk(rq_s[rs, :], rqw_s[rs, :], rk_s[rs, :], rkw_s[rs, :], rv_s[rs, :],
                               sret_ref[...], intra_decay, chunk_decay)
        ro_s[rs, :] = o
        sret_ref[...] = sr_new
        return carry

    lax.fori_loop(0, rows // CHUNK, chunk_body, 0)

    y = y_s[...]
    inv_n = 1.0 / RWKV_HEAD_DIM
    yc = y - _x_const_mm(y, hones, 2) * inv_n
    var = _x_const_mm(yc * yc, hones, 2) * inv_n
    y = (yc * lax.rsqrt(var + LN_X_EPS) * ln_w + ln_b + bonus) * g
    o = ro_s[...]
    o = jnp.concatenate(
        [o[:, i * RET_V_DIM:(i + 1) * RET_V_DIM]
         * lax.rsqrt(jnp.mean(jnp.square(o[:, i * RET_V_DIM:(i + 1) * RET_V_DIM]),
                              axis=-1, keepdims=True) + NORM_EPS)
         for i in range(RET_HEADS)], axis=1)
    o = o * (rg * _sigmoid(rg))
    mixed = jnp.concatenate([y, o], axis=1).astype(BF16)
    o_ref[0] = x + _dot(mixed, wout_ref[...])


def _ffn_kernel(x_ref, gain_ref, wg_ref, wu_ref, wd_ref, fin_ref, o_ref, *, final_norm):
    x = x_ref[...]
    h = (x * lax.rsqrt(jnp.mean(x * x, axis=-1, keepdims=True) + NORM_EPS)
         * gain_ref[...]).astype(BF16)
    gate = _dot(h, wg_ref[...])
    up = _dot(h, wu_ref[...])
    act = (gate * _sigmoid(gate) * up).astype(BF16)
    y = x + 0.5 * _dot(act, wd_ref[...])
    if final_norm:
        y = y * lax.rsqrt(jnp.mean(y * y, axis=-1, keepdims=True) + NORM_EPS) * fin_ref[...]
    o_ref[...] = y


def _resident(shape):
    return pl.BlockSpec(shape, lambda *_: (0,) * len(shape), pipeline_mode=pl.Buffered(1))


def _ffn(x2, gain, wg, wu, wd, fin, final_norm):
    t = x2.shape[0]
    rows = min(FFN_ROWS, t)
    assert t % rows == 0
    return pl.pallas_call(
        functools.partial(_ffn_kernel, final_norm=final_norm),
        out_shape=jax.ShapeDtypeStruct(x2.shape, F32),
        grid=(t // rows,),
        in_specs=[pl.BlockSpec((rows, D_MODEL), lambda i: (i, 0)),
                  _resident((1, D_MODEL)), _resident((D_MODEL, D_FF)), _resident((D_MODEL, D_FF)),
                  _resident((D_FF, D_MODEL)), _resident((1, D_MODEL))],
        out_specs=pl.BlockSpec((rows, D_MODEL), lambda i: (i, 0)),
        compiler_params=pltpu.CompilerParams(
            dimension_semantics=("arbitrary",), vmem_limit_bytes=VMEM_LIMIT_BYTES),
        name="ffn_final" if final_norm else "ffn",
    )(x2, gain, wg, wu, wd, fin)


def _mixer_tables(seq, rows):
    idx = jnp.arange(rows)
    same_chunk = (idx[:, None] // CHUNK) == (idx[None, :] // CHUNK)
    tri = (same_chunk & (idx[None, :] <= idx[:, None])).astype(BF16)
    ones = same_chunk.astype(BF16)
    hd = jnp.arange(RWKV_WIDTH) // RWKV_HEAD_DIM
    hones = (hd[:, None] == hd[None, :]).astype(BF16)
    half = RET_QK_DIM // 2
    inv_freq = 1.0 / (ROPE_BASE ** jnp.linspace(0.0, 1.0, half, dtype=F32))
    ang = jnp.arange(seq, dtype=F32)[:, None] * inv_freq[None, :]
    cos = jnp.tile(jnp.cos(ang), (1, 2 * RET_HEADS))
    sin = jnp.tile(jnp.concatenate([-jnp.sin(ang), jnp.sin(ang)], axis=1), (1, RET_HEADS))
    log_gamma = jnp.log(1.0 - jnp.power(2.0, -5.0 - jnp.arange(RET_HEADS, dtype=F32)))
    pos = jnp.arange(CHUNK, dtype=F32)
    dist = jnp.abs(pos[:, None] - pos[None, :])
    dd = jnp.exp(log_gamma[:, None, None] * dist)
    dd = jnp.transpose(dd, (1, 0, 2)).reshape(CHUNK, RET_HEADS * CHUNK)
    query_w = jnp.exp(log_gamma[:, None] * (pos + 1.0)[None, :])
    key_w = jnp.exp(log_gamma[:, None] * (CHUNK - 1.0 - pos)[None, :])
    qw = jnp.tile(jnp.repeat(query_w.T, RET_QK_DIM, axis=1), (rows // CHUNK, 1))
    kw = jnp.tile(jnp.repeat(key_w.T, RET_QK_DIM, axis=1), (rows // CHUNK, 1))
    g64 = jnp.repeat(jnp.exp(log_gamma * CHUNK), RET_V_DIM)[None, :]
    return tri, ones, hones, cos, sin, dd, qw, kw, g64


def _mixer(x, gain, w_in, mu, lora, glora, vecs, w_out, tables, inv_passes=3):
    b, seq, _ = x.shape
    rows = min(MIX_ROWS, seq)
    assert seq % rows == 0 and rows % CHUNK == 0
    tri, ones, hones, cos, sin, dd, qw, kw, g64 = tables
    f32_scratch = lambda width: pltpu.VMEM((rows, width), F32)
    return pl.pallas_call(
        functools.partial(_mixer_kernel, rows=rows, inv_passes=inv_passes),
        out_shape=jax.ShapeDtypeStruct(x.shape, F32),
        grid=(b, seq // rows),
        in_specs=[pl.BlockSpec((1, rows, D_MODEL), lambda i, j: (i, j, 0)),
                  _resident((1, D_MODEL)), _resident((D_MODEL, PROJ_WIDTH)),
                  _resident((1, RWKV_IN)), _resident((DECAY_LORA + ICL_LORA, 2 * RWKV_WIDTH)),
                  _resident((GATE_LORA, RWKV_WIDTH)), _resident((8, RWKV_WIDTH)),
                  _resident((D_MODEL, D_MODEL)),
                  _resident((rows, rows)), _resident((rows, rows)),
                  _resident((RWKV_WIDTH, RWKV_WIDTH)),
                  pl.BlockSpec((rows, RET_QK_WIDTH), lambda i, j: (j, 0)),
                  pl.BlockSpec((rows, RET_QK_WIDTH), lambda i, j: (j, 0)),
                  _resident((CHUNK, RET_QK_WIDTH)), _resident((rows, RET_QK_WIDTH)),
                  _resident((rows, RET_QK_WIDTH)), _resident((1, RET_V_WIDTH))],
        out_specs=pl.BlockSpec((1, rows, D_MODEL), lambda i, j: (i, j, 0)),
        scratch_shapes=[pltpu.VMEM((1, RWKV_IN), F32),
                        pltpu.VMEM((RWKV_HEADS // GROUP, QUAD, QUAD), F32),
                        pltpu.VMEM((RET_QK_WIDTH, RET_V_WIDTH), F32)]
                       + [f32_scratch(RWKV_WIDTH)] * 9
                       + [f32_scratch(RET_QK_WIDTH)] * 4 + [f32_scratch(RET_V_WIDTH)] * 2,
        compiler_params=pltpu.CompilerParams(
            dimension_semantics=("arbitrary", "arbitrary"), vmem_limit_bytes=VMEM_LIMIT_BYTES),
        name="mixer",
    )(x, gain, w_in, mu, lora, glora, vecs, w_out, tri, ones, hones, cos, sin, dd, qw, kw, g64)


def kernel(x, ffn1_norm, ffn1_w_gate, ffn1_w_up, ffn1_w_down, mix_norm, w_in, shift_mu, w0, w_lora_up, a0, a_lora_up, g_lora_up, k_k, k_a, r_k, ln_x_w, ln_x_b, w_out, ffn2_norm, ffn2_w_gate, ffn2_w_up, ffn2_w_down, final_norm):
    b, seq, d = x.shape
    depth = w_in.shape[0]
    tables = _mixer_tables(seq, min(MIX_ROWS, seq))
    fin = final_norm[None, :]
    zeros = jnp.zeros((DECAY_LORA, RWKV_WIDTH), F32)
    for l in range(depth):
        x = _ffn(x.reshape(b * seq, d), ffn1_norm[l][None, :], ffn1_w_gate[l].astype(BF16),
                 ffn1_w_up[l].astype(BF16), ffn1_w_down[l].astype(BF16), fin, False)
        lora = jnp.concatenate(
            [jnp.concatenate([w_lora_up[l], zeros], axis=1),
             jnp.concatenate([zeros, a_lora_up[l]], axis=1)], axis=0).astype(BF16)
        vecs = jnp.stack([w0[l], a0[l], k_k[l], k_a[l], r_k[l].reshape(-1), ln_x_w[l], ln_x_b[l],
                          jnp.zeros((RWKV_WIDTH,), F32)], axis=0)
        x = _mixer(x.reshape(b, seq, d), mix_norm[l][None, :], w_in[l].astype(BF16),
                   shift_mu[l][None, :], lora, g_lora_up[l].astype(BF16), vecs,
                   w_out[l].astype(BF16), tables)
        x = _ffn(x.reshape(b * seq, d), ffn2_norm[l][None, :], ffn2_w_gate[l].astype(BF16),
                 ffn2_w_up[l].astype(BF16), ffn2_w_down[l].astype(BF16), fin, l == depth - 1)
    return x.reshape(b, seq, d)
```

```python
import functools

import jax
import jax.numpy as jnp
from jax import lax
from jax.experimental import pallas as pl
from jax.experimental.pallas import tpu as pltpu

D_MODEL = 1024
D_FF = 2816
CHUNK = 64
RWKV_HEADS = 8
RWKV_HEAD_DIM = 64
RWKV_WIDTH = RWKV_HEADS * RWKV_HEAD_DIM
DECAY_LORA = 64
ICL_LORA = 64
GATE_LORA = 128
RWKV_IN = 3 * RWKV_WIDTH + DECAY_LORA + ICL_LORA + GATE_LORA
RET_HEADS = 4
RET_QK_DIM = 64
RET_V_DIM = 128
RET_QK_WIDTH = RET_HEADS * RET_QK_DIM
RET_V_WIDTH = RET_HEADS * RET_V_DIM
RET_IN = 2 * RET_QK_WIDTH + 2 * RET_V_WIDTH
PROJ_WIDTH = RWKV_IN + RET_IN
ROPE_BASE = 10000.0
NORM_EPS = 1e-6
LN_X_EPS = 64e-5

GROUP = 4
QUAD = GROUP * RWKV_HEAD_DIM
N_QUADS = RWKV_HEADS // GROUP
FFN_ROWS = 512
MIX_ROWS = 256
VMEM_LIMIT_BYTES = 58 * 1024 * 1024

F32 = jnp.float32
BF16 = jnp.bfloat16
NN = (((1,), (0,)), ((), ()))
NT = (((1,), (1,)), ((), ()))
TN = (((0,), (0,)), ((), ()))


def _dot(a, b, dims=NN):
    return lax.dot_general(a, b, dims, preferred_element_type=F32)


def _mm(a, b, dims=NN):
    return _dot(a.astype(BF16), b.astype(BF16), dims)


def _const_mm(c_bf16, x, terms):
    acc = None
    rem = x
    for _ in range(terms):
        piece = rem.astype(BF16)
        rem = rem - piece.astype(F32)
        d = _dot(c_bf16, piece)
        acc = d if acc is None else acc + d
    return acc


def _x_const_mm(x, c_bf16, terms):
    acc = None
    rem = x
    for _ in range(terms):
        piece = rem.astype(BF16)
        rem = rem - piece.astype(F32)
        d = _dot(piece, c_bf16)
        acc = d if acc is None else acc + d
    return acc


def _head_sum(z, quad_ones):
    return jnp.concatenate(
        [_x_const_mm(z[:, i * QUAD:(i + 1) * QUAD], quad_ones, 2) for i in range(N_QUADS)], axis=1)


def _sigmoid(x):
    return 1.0 / (1.0 + jnp.exp(-x))


def _softplus(x):
    return jnp.maximum(x, 0.0) + jnp.log(1.0 + jnp.exp(-jnp.abs(x)))


def _bd_rows(x, group_lanes):
    lane_head = lax.broadcasted_iota(jnp.int32, x.shape, 1) // group_lanes
    return jnp.concatenate(
        [jnp.where(lane_head == h, x, 0.0) for h in range(GROUP)], axis=0)


def _mask_bd(z, group_lanes):
    lane_head = lax.broadcasted_iota(jnp.int32, (CHUNK, z.shape[1]), 1) // group_lanes
    return jnp.concatenate(
        [jnp.where(lane_head == h, z[h * CHUNK:(h + 1) * CHUNK], 0.0) for h in range(GROUP)],
        axis=0)


def _each(fn, *lists):
    return [fn(*args) for args in zip(*lists)]


def _prod(a_list, b_list):
    return _each(lambda a, b: _mm(a, _bd_rows(b, CHUNK)), a_list, b_list)


def _add(a_list, b_list):
    return _each(lambda a, b: a + b, a_list, b_list)


def _tri_inverse(ns):
    shape = ns[0].shape
    row = lax.broadcasted_iota(jnp.int32, shape, 0)
    col = lax.broadcasted_iota(jnp.int32, shape, 1) % CHUNK
    same16 = (row // 16) == (col // 16)
    same32 = (row // 32) == (col // 32)
    n0 = [jnp.where(same16, n, 0.0) for n in ns]
    n1 = [jnp.where(same32, jnp.where(same16, 0.0, n), 0.0) for n in ns]
    n2 = [jnp.where(same32, 0.0, n) for n in ns]
    eye = jnp.where(row == col, 1.0, 0.0)
    t = [eye + n for n in n0]
    m = _prod(n0, n0)
    t = _add(t, _prod(t, m))
    m = _prod(m, m)
    t = _add(t, _prod(t, m))
    m = _prod(m, m)
    t = _add(t, _prod(t, m))
    t = _add(t, _prod(_prod(t, n1), t))
    t = _add(t, _prod(_prod(t, n2), t))
    return t


def _rwkv_local(at, rt, bt, kt, v):
    a = _each(lambda at_, rt_, bt_, kt_: _mm(
        jnp.concatenate([at_, rt_], axis=0),
        jnp.concatenate([_bd_rows(bt_, CHUNK), _bd_rows(kt_, CHUNK)], axis=0), NT),
        at, rt, bt, kt)
    row = lax.broadcasted_iota(jnp.int32, (CHUNK, 2 * QUAD), 0)
    col = lax.broadcasted_iota(jnp.int32, (CHUNK, 2 * QUAD), 1) % CHUNK
    a_a = [jnp.where(col < row, x[:CHUNK], 0.0) for x in a]
    a_r = [jnp.where(col <= row, x[CHUNK:], 0.0) for x in a]
    bd_v = [_bd_rows(x, CHUNK) for x in v]
    g2 = _each(lambda x, y: _mm(x[:, QUAD:], y), a_a, bd_v)
    t = _tri_inverse([x[:, :QUAD] for x in a_a])
    tu = _each(lambda t_, at_, g2_: _mm(
        t_, jnp.concatenate([_bd_rows(at_, CHUNK), _bd_rows(g2_, CHUNK)], axis=1)), t, at, g2)
    return [x[:, :QUAD] for x in tu], [x[:, QUAD:] for x in tu], a_r, bd_v


def _rwkv_state_step(a_hat, u_loc, a_r, bd_v, rt, bw, kw, v, s, w_end):
    ps = _each(lambda a_, r_, s_: _mm(jnp.concatenate([a_, r_], axis=0), s_, NT),
               a_hat, rt, s)
    u = _each(lambda p_, u_: p_[:CHUNK] + u_, ps, u_loc)
    upd = _each(lambda u_, v_, bw_, kw_: _mm(jnp.concatenate([u_, v_], axis=0),
                                             jnp.concatenate([bw_, kw_], axis=0), TN),
                u, v, bw, kw)
    s_new = _each(lambda s_, w_, d_: s_ * w_ + _mask_bd(d_, CHUNK), s, w_end, upd)
    y = _each(lambda p_, a_, u_, b_: p_[CHUNK:] + _mm(
        a_, jnp.concatenate([_bd_rows(u_, CHUNK), b_], axis=0)), ps, a_r, u, bd_v)
    return y, s_new


def _ret_local(q, k, k_w, v, intra_decay):
    scores = _each(lambda q_, k_: _mm(q_, _bd_rows(k_, RET_QK_DIM), NT) * intra_decay, q, k)
    intra = _each(lambda s_, v_: _mm(s_, _bd_rows(v_, RET_V_DIM)), scores, v)
    kv = _each(lambda k_, v_: _mask_bd(_mm(k_, v_, TN), RET_V_DIM), k_w, v)
    return intra, kv


def _rope(x, cos, sin_signed):
    half = RET_QK_DIM // 2
    width = x.shape[1]
    lane = lax.broadcasted_iota(jnp.int32, x.shape, 1) % RET_QK_DIM
    partner = jnp.where(lane < half, pltpu.roll(x, width - half, 1), pltpu.roll(x, half, 1))
    return x * cos + partner * sin_signed


def _mixer_kernel(x_ref, gain_ref, win_ref, mu_ref, lora_ref, glora_ref, vecs_ref, wout_ref,
                  tri_ref, ones_ref, qones_ref, cos_ref, sin_ref, dd_ref, qw_ref, kw_ref, g64_ref,
                  o_ref, carry_ref, srw_ref, sret_ref, *, rows):
    @pl.when(pl.program_id(1) == 0)
    def _():
        carry_ref[...] = jnp.zeros_like(carry_ref)
        srw_ref[...] = jnp.zeros_like(srw_ref)
        sret_ref[...] = jnp.zeros_like(sret_ref)

    x = x_ref[0]
    h = x * lax.rsqrt(jnp.mean(x * x, axis=-1, keepdims=True) + NORM_EPS) * gain_ref[...]
    p = _dot(h.astype(BF16), win_ref[...])

    pr = p[:, :RWKV_IN]
    first = lax.broadcasted_iota(jnp.int32, pr.shape, 0) == 0
    prev = jnp.where(first, carry_ref[...], pltpu.roll(pr, 1, 0))
    carry_ref[...] = pr[rows - 1:rows, :]
    ps = pr + mu_ref[...] * (prev - pr)
    w = RWKV_WIDTH
    r, k, v = ps[:, :w], ps[:, w:2 * w], ps[:, 2 * w:3 * w]
    wa = ps[:, 3 * w:3 * w + DECAY_LORA + ICL_LORA]
    gd = ps[:, 3 * w + DECAY_LORA + ICL_LORA:]
    is_decay = lax.broadcasted_iota(jnp.int32, wa.shape, 1) < DECAY_LORA
    lora = _dot(jnp.where(is_decay, jnp.tanh(wa), wa).astype(BF16), lora_ref[...])
    vecs = vecs_ref[...]
    w0, a0, k_k, k_a, r_k, ln_w, ln_b = (vecs[i:i + 1] for i in range(7))
    log_w = -_softplus(-(w0 + lora[:, :w])) - 0.5
    log_decay = -jnp.exp(log_w)
    a = _sigmoid(a0 + lora[:, w:])
    g = _dot(_sigmoid(gd).astype(BF16), glora_ref[...])
    qones = qones_ref[...]
    kk = k * k_k
    kk = kk * lax.rsqrt(jnp.maximum(_head_sum(kk * kk, qones), 1e-24))
    k_mod = k * (1.0 + (a - 1.0) * k_a)
    bonus = _head_sum(r * k_mod * r_k, qones) * v
    cum = _const_mm(tri_ref[...], log_decay, 3)
    tot = _const_mm(ones_ref[...], log_decay, 3)
    w_inv = jnp.exp(-cum)
    w_end = jnp.exp(tot - cum)
    kka = kk * a
    at_all = -kk * jnp.exp(cum - log_decay)
    rt_all = r * jnp.exp(cum)
    bt_all = kka * w_inv
    kt_all = k_mod * w_inv
    bw_all = kka * w_end
    kw_all = k_mod * w_end
    w_tot = jnp.exp(tot)

    q0 = RWKV_IN
    cos = cos_ref[...]
    sin = sin_ref[...]
    rq = _rope(p[:, q0:q0 + RET_QK_WIDTH], cos, sin)
    rk = _rope(p[:, q0 + RET_QK_WIDTH:q0 + 2 * RET_QK_WIDTH], cos, sin) * (RET_QK_DIM ** -0.5)
    rqw = rq * qw_ref[...]
    rkw = rk * kw_ref[...]
    rv = p[:, q0 + 2 * RET_QK_WIDTH:q0 + 2 * RET_QK_WIDTH + RET_V_WIDTH]
    rg = p[:, q0 + 2 * RET_QK_WIDTH + RET_V_WIDTH:]
    intra_decay = dd_ref[...]
    chunk_decay = g64_ref[...]

    n_chunks = rows // CHUNK
    def pieces(z):
        return [z[c * CHUNK:(c + 1) * CHUNK, qd * QUAD:(qd + 1) * QUAD]
                for c in range(n_chunks) for qd in range(N_QUADS)]

    def rows_of(z):
        return [z[c * CHUNK:(c + 1) * CHUNK] for c in range(n_chunks)]

    at_p, rt_p, v_p = pieces(at_all), pieces(rt_all), pieces(v)
    a_hat, u_loc, a_r, bd_v = _rwkv_local(at_p, rt_p, pieces(bt_all), pieces(kt_all), v_p)
    intra, kv = _ret_local(rows_of(rq), rows_of(rk), rows_of(rkw), rows_of(rv), intra_decay)
    bw_p, kw_p = pieces(bw_all), pieces(kw_all)
    w_p = [w_tot[c * CHUNK:c * CHUNK + 1, qd * QUAD:(qd + 1) * QUAD]
           for c in range(n_chunks) for qd in range(N_QUADS)]
    rqw_p = rows_of(rqw)

    states = [srw_ref[qd] for qd in range(N_QUADS)]
    s_ret = sret_ref[...]
    y_rows, o_rows = [], []
    for c in range(n_chunks):
        cs = slice(c * N_QUADS, (c + 1) * N_QUADS)
        y_quads, states = _rwkv_state_step(a_hat[cs], u_loc[cs], a_r[cs], bd_v[cs], rt_p[cs],
                                           bw_p[cs], kw_p[cs], v_p[cs], states, w_p[cs])
        y_rows.append(jnp.concatenate(y_quads, axis=1))
        o_rows.append(intra[c] + _mm(rqw_p[c], s_ret))
        s_ret = s_ret * chunk_decay + kv[c]
    for qd in range(N_QUADS):
        srw_ref[qd] = states[qd]
    sret_ref[...] = s_ret

    y = jnp.concatenate(y_rows, axis=0)
    inv_n = 1.0 / RWKV_HEAD_DIM
    yc = y - _head_sum(y, qones) * inv_n
    var = _head_sum(yc * yc, qones) * inv_n
    y = (yc * lax.rsqrt(var + LN_X_EPS) * ln_w + ln_b + bonus) * g
    o = jnp.concatenate(o_rows, axis=0)
    o = jnp.concatenate(
        [o[:, i * RET_V_DIM:(i + 1) * RET_V_DIM]
         * lax.rsqrt(jnp.mean(jnp.square(o[:, i * RET_V_DIM:(i + 1) * RET_V_DIM]),
                              axis=-1, keepdims=True) + NORM_EPS)
         for i in range(RET_HEADS)], axis=1)
    o = o * (rg * _sigmoid(rg))
    mixed = jnp.concatenate([y, o], axis=1).astype(BF16)
    o_ref[0] = x + _dot(mixed, wout_ref[...])


def _ffn_kernel(x_ref, gain_ref, wg_ref, wu_ref, wd_ref, fin_ref, o_ref, *, final_norm):
    x = x_ref[...]
    h = (x * lax.rsqrt(jnp.mean(x * x, axis=-1, keepdims=True) + NORM_EPS)
         * gain_ref[...]).astype(BF16)
    gate = _dot(h, wg_ref[...])
    up = _dot(h, wu_ref[...])
    act = (gate * _sigmoid(gate) * up).astype(BF16)
    y = x + 0.5 * _dot(act, wd_ref[...])
    if final_norm:
        y = y * lax.rsqrt(jnp.mean(y * y, axis=-1, keepdims=True) + NORM_EPS) * fin_ref[...]
    o_ref[...] = y


def _resident(shape):
    return pl.BlockSpec(shape, lambda *_: (0,) * len(shape), pipeline_mode=pl.Buffered(1))


def _ffn(x2, gain, wg, wu, wd, fin, final_norm):
    t = x2.shape[0]
    rows = min(FFN_ROWS, t)
    assert t % rows == 0
    return pl.pallas_call(
        functools.partial(_ffn_kernel, final_norm=final_norm),
        out_shape=jax.ShapeDtypeStruct(x2.shape, F32),
        grid=(t // rows,),
        in_specs=[pl.BlockSpec((rows, D_MODEL), lambda i: (i, 0)),
                  _resident((1, D_MODEL)), _resident((D_MODEL, D_FF)), _resident((D_MODEL, D_FF)),
                  _resident((D_FF, D_MODEL)), _resident((1, D_MODEL))],
        out_specs=pl.BlockSpec((rows, D_MODEL), lambda i: (i, 0)),
        compiler_params=pltpu.CompilerParams(
            dimension_semantics=("arbitrary",), vmem_limit_bytes=VMEM_LIMIT_BYTES),
        name="ffn_final" if final_norm else "ffn",
    )(x2, gain, wg, wu, wd, fin)


def _mixer_tables(seq, rows):
    idx = jnp.arange(rows)
    same_chunk = (idx[:, None] // CHUNK) == (idx[None, :] // CHUNK)
    tri = (same_chunk & (idx[None, :] <= idx[:, None])).astype(BF16)
    ones = same_chunk.astype(BF16)
    hd = jnp.arange(QUAD) // RWKV_HEAD_DIM
    qones = (hd[:, None] == hd[None, :]).astype(BF16)
    half = RET_QK_DIM // 2
    inv_freq = 1.0 / (ROPE_BASE ** jnp.linspace(0.0, 1.0, half, dtype=F32))
    ang = jnp.arange(seq, dtype=F32)[:, None] * inv_freq[None, :]
    cos = jnp.tile(jnp.cos(ang), (1, 2 * RET_HEADS))
    sin = jnp.tile(jnp.concatenate([-jnp.sin(ang), jnp.sin(ang)], axis=1), (1, RET_HEADS))
    log_gamma = jnp.log(1.0 - jnp.power(2.0, -5.0 - jnp.arange(RET_HEADS, dtype=F32)))
    pos = jnp.arange(CHUNK, dtype=F32)
    dist = jnp.abs(pos[:, None] - pos[None, :])
    dd = jnp.exp(log_gamma[:, None, None] * dist)
    dd = jnp.transpose(dd, (1, 0, 2)).reshape(CHUNK, RET_HEADS * CHUNK)
    query_w = jnp.exp(log_gamma[:, None] * (pos + 1.0)[None, :])
    key_w = jnp.exp(log_gamma[:, None] * (CHUNK - 1.0 - pos)[None, :])
    qw = jnp.tile(jnp.repeat(query_w.T, RET_QK_DIM, axis=1), (rows // CHUNK, 1))
    kw = jnp.tile(jnp.repeat(key_w.T, RET_QK_DIM, axis=1), (rows // CHUNK, 1))
    g64 = jnp.repeat(jnp.exp(log_gamma * CHUNK), RET_V_DIM)[None, :]
    return tri, ones, qones, cos, sin, dd, qw, kw, g64


def _mixer(x, gain, w_in, mu, lora, glora, vecs, w_out, tables):
    b, seq, _ = x.shape
    rows = min(MIX_ROWS, seq)
    assert seq % rows == 0 and rows % CHUNK == 0
    tri, ones, qones, cos, sin, dd, qw, kw, g64 = tables
    return pl.pallas_call(
        functools.partial(_mixer_kernel, rows=rows),
        out_shape=jax.ShapeDtypeStruct(x.shape, F32),
        grid=(b, seq // rows),
        in_specs=[pl.BlockSpec((1, rows, D_MODEL), lambda i, j: (i, j, 0)),
                  _resident((1, D_MODEL)), _resident((D_MODEL, PROJ_WIDTH)),
                  _resident((1, RWKV_IN)), _resident((DECAY_LORA + ICL_LORA, 2 * RWKV_WIDTH)),
                  _resident((GATE_LORA, RWKV_WIDTH)), _resident((8, RWKV_WIDTH)),
                  _resident((D_MODEL, D_MODEL)),
                  _resident((rows, rows)), _resident((rows, rows)), _resident((QUAD, QUAD)),
                  pl.BlockSpec((rows, RET_QK_WIDTH), lambda i, j: (j, 0)),
                  pl.BlockSpec((rows, RET_QK_WIDTH), lambda i, j: (j, 0)),
                  _resident((CHUNK, RET_QK_WIDTH)), _resident((rows, RET_QK_WIDTH)),
                  _resident((rows, RET_QK_WIDTH)), _resident((1, RET_V_WIDTH))],
        out_specs=pl.BlockSpec((1, rows, D_MODEL), lambda i, j: (i, j, 0)),
        scratch_shapes=[pltpu.VMEM((1, RWKV_IN), F32),
                        pltpu.VMEM((N_QUADS, QUAD, QUAD), F32),
                        pltpu.VMEM((RET_QK_WIDTH, RET_V_WIDTH), F32)],
        compiler_params=pltpu.CompilerParams(
            dimension_semantics=("arbitrary", "arbitrary"), vmem_limit_bytes=VMEM_LIMIT_BYTES),
        name="mixer",
    )(x, gain, w_in, mu, lora, glora, vecs, w_out, tri, ones, qones, cos, sin, dd, qw, kw, g64)


def kernel(x, ffn1_norm, ffn1_w_gate, ffn1_w_up, ffn1_w_down, mix_norm, w_in, shift_mu, w0, w_lora_up, a0, a_lora_up, g_lora_up, k_k, k_a, r_k, ln_x_w, ln_x_b, w_out, ffn2_norm, ffn2_w_gate, ffn2_w_up, ffn2_w_down, final_norm):
    b, seq, d = x.shape
    depth = w_in.shape[0]
    tables = _mixer_tables(seq, min(MIX_ROWS, seq))
    fin = final_norm[None, :]
    zeros = jnp.zeros((DECAY_LORA, RWKV_WIDTH), F32)
    for l in range(depth):
        x = _ffn(x.reshape(b * seq, d), ffn1_norm[l][None, :], ffn1_w_gate[l].astype(BF16),
                 ffn1_w_up[l].astype(BF16), ffn1_w_down[l].astype(BF16), fin, False)
        lora = jnp.concatenate(
            [jnp.concatenate([w_lora_up[l], zeros], axis=1),
             jnp.concatenate([zeros, a_lora_up[l]], axis=1)], axis=0).astype(BF16)
        vecs = jnp.stack([w0[l], a0[l], k_k[l], k_a[l], r_k[l].reshape(-1), ln_x_w[l], ln_x_b[l],
                          jnp.zeros((RWKV_WIDTH,), F32)], axis=0)
        x = _mixer(x.reshape(b, seq, d), mix_norm[l][None, :], w_in[l].astype(BF16),
                   shift_mu[l][None, :], lora, g_lora_up[l].astype(BF16), vecs,
                   w_out[l].astype(BF16), tables)
        x = _ffn(x.reshape(b * seq, d), ffn2_norm[l][None, :], ffn2_w_gate[l].astype(BF16),
                 ffn2_w_up[l].astype(BF16), ffn2_w_down[l].astype(BF16), fin, l == depth - 1)
    return x.reshape(b, seq, d)
```

```python
import functools

import jax
import jax.numpy as jnp
from jax import lax
from jax.experimental import pallas as pl
from jax.experimental.pallas import tpu as pltpu

D_MODEL = 1024
D_FF = 2816
CHUNK = 64
RWKV_HEADS = 8
RWKV_HEAD_DIM = 64
RWKV_WIDTH = RWKV_HEADS * RWKV_HEAD_DIM
DECAY_LORA = 64
ICL_LORA = 64
GATE_LORA = 128
RWKV_IN = 3 * RWKV_WIDTH + DECAY_LORA + ICL_LORA + GATE_LORA
RET_HEADS = 4
RET_QK_DIM = 64
RET_V_DIM = 128
RET_QK_WIDTH = RET_HEADS * RET_QK_DIM
RET_V_WIDTH = RET_HEADS * RET_V_DIM
RET_IN = 2 * RET_QK_WIDTH + 2 * RET_V_WIDTH
PROJ_WIDTH = RWKV_IN + RET_IN
ROPE_BASE = 10000.0
NORM_EPS = 1e-6
LN_X_EPS = 64e-5

GROUP = 4
QUAD = GROUP * RWKV_HEAD_DIM
N_QUADS = RWKV_HEADS // GROUP
MXU_TILE = 256
FFN_ROWS = 512
MIX_ROWS = 256
MIX_SEQS = 2
VMEM_LIMIT_BYTES = 58 * 1024 * 1024

F32 = jnp.float32
BF16 = jnp.bfloat16
NN = (((1,), (0,)), ((), ()))
NT = (((1,), (1,)), ((), ()))
TN = (((0,), (0,)), ((), ()))


def _dot(a, b, dims=NN):
    return lax.dot_general(a, b, dims, preferred_element_type=F32)


def _mm(a, b, dims=NN):
    return _dot(a.astype(BF16), b.astype(BF16), dims)


def _const_mm(c_bf16, x, terms):
    acc = None
    rem = x
    for _ in range(terms):
        piece = rem.astype(BF16)
        rem = rem - piece.astype(F32)
        d = _dot(c_bf16, piece)
        acc = d if acc is None else acc + d
    return acc


def _head_sum(z, quad_ones):
    del quad_ones
    lanes = 2 * RWKV_HEAD_DIM
    low = lax.broadcasted_iota(jnp.int32, (z.shape[0], lanes), 1) < RWKV_HEAD_DIM
    out = []
    for i in range(z.shape[1] // lanes):
        t = z[:, i * lanes:(i + 1) * lanes]
        s_lo = jnp.sum(jnp.where(low, t, 0.0), axis=-1, keepdims=True)
        s_hi = jnp.sum(jnp.where(low, 0.0, t), axis=-1, keepdims=True)
        out.append(jnp.where(low, s_lo, s_hi))
    return jnp.concatenate(out, axis=1)


def _sigmoid(x):
    return 1.0 / (1.0 + jnp.exp(-x))


def _softplus(x):
    return jnp.maximum(x, 0.0) + jnp.log(1.0 + jnp.exp(-jnp.abs(x)))


def _rms_norm(x, gain):
    return x * lax.rsqrt(jnp.mean(x * x, axis=-1, keepdims=True) + NORM_EPS) * gain


def _bd_rows(x, group_lanes):
    lane_head = lax.broadcasted_iota(jnp.int32, x.shape, 1) // group_lanes
    return jnp.concatenate(
        [jnp.where(lane_head == h, x, 0.0) for h in range(GROUP)], axis=0)


def _mask_bd(z, group_lanes):
    lane_head = lax.broadcasted_iota(jnp.int32, (CHUNK, z.shape[1]), 1) // group_lanes
    return jnp.concatenate(
        [jnp.where(lane_head == h, z[h * CHUNK:(h + 1) * CHUNK], 0.0) for h in range(GROUP)],
        axis=0)


class _Fill:
    def __init__(self, pieces=()):
        self.pieces = list(pieces)

    def __call__(self):
        if self.pieces:
            self.pieces.pop(0)()

    def drain(self):
        while self.pieces:
            self.pieces.pop(0)()


def _each(fn, *lists):
    return [fn(*args) for args in zip(*lists)]


def _prod(a_list, b_list, fill):
    out = _each(lambda a, b: _mm(a, _bd_rows(b, CHUNK)), a_list, b_list)
    fill()
    return out


def _add(a_list, b_list):
    return _each(lambda a, b: a + b, a_list, b_list)


def _tri_inverse(ns, fill):
    shape = ns[0].shape
    row = lax.broadcasted_iota(jnp.int32, shape, 0)
    col = lax.broadcasted_iota(jnp.int32, shape, 1) % CHUNK
    same16 = (row // 16) == (col // 16)
    same32 = (row // 32) == (col // 32)
    n0 = [jnp.where(same16, n, 0.0) for n in ns]
    n1 = [jnp.where(same32, jnp.where(same16, 0.0, n), 0.0) for n in ns]
    n2 = [jnp.where(same32, 0.0, n) for n in ns]
    eye = jnp.where(row == col, 1.0, 0.0)
    t = [eye + n for n in n0]
    m = _prod(n0, n0, fill)
    for _ in range(2):
        tm = _prod(_each(lambda a, b: jnp.concatenate([a, b], axis=0), t, m), m, fill)
        t = _each(lambda a, b: a + b[:CHUNK], t, tm)
        m = [x[CHUNK:] for x in tm]
    t = _add(t, _prod(t, m, fill))
    stack = lambda a, b: jnp.concatenate([a, b], axis=0)
    nt = _prod(_each(stack, n1, n2), t, fill)
    x1 = [x[:CHUNK] for x in nt]
    tz = _prod(_each(lambda a, b: stack(a, b[CHUNK:]), t, nt), x1, fill)
    t = _each(lambda a, b: a + b[:CHUNK], t, tz)
    n2t = _each(lambda a, b: a[CHUNK:] + b[CHUNK:], nt, tz)
    return _add(t, _prod(t, n2t, fill))


def _rwkv_local(at, rt, bt, kt, v, fill):
    a = _each(lambda at_, rt_, bt_, kt_: _mm(
        jnp.concatenate([at_, rt_], axis=0),
        jnp.concatenate([_bd_rows(bt_, CHUNK), _bd_rows(kt_, CHUNK)], axis=0), NT),
        at, rt, bt, kt)
    fill()
    row = lax.broadcasted_iota(jnp.int32, (CHUNK, 2 * QUAD), 0)
    col = lax.broadcasted_iota(jnp.int32, (CHUNK, 2 * QUAD), 1) % CHUNK
    a_a = [jnp.where(col < row, x[:CHUNK], 0.0) for x in a]
    a_r = [jnp.where(col <= row, x[CHUNK:], 0.0) for x in a]
    av = _each(lambda x, y, v_: _mm(jnp.concatenate([x[:, QUAD:], y[:, QUAD:]], axis=0),
                                    _bd_rows(v_, CHUNK)), a_a, a_r, v)
    fill()
    t = _tri_inverse([x[:, :QUAD] for x in a_a], fill)
    tu = _each(lambda t_, at_, av_: _mm(
        t_, jnp.concatenate([_bd_rows(at_, CHUNK), _bd_rows(av_[:CHUNK], CHUNK)], axis=1)),
        t, at, av)
    fill()
    return ([x[:, :QUAD] for x in tu], [x[:, QUAD:] for x in tu],
            [x[:, :QUAD] for x in a_r], [x[CHUNK:] for x in av])


def _ret_local(q, k, k_w, v, intra_decay, fill):
    scores = _each(lambda q_, k_: _mm(q_, _bd_rows(k_, RET_QK_DIM), NT) * intra_decay, q, k)
    fill()
    intra = _each(lambda s_, v_: _mm(s_, _bd_rows(v_, RET_V_DIM)), scores, v)
    fill()
    kv = _each(lambda k_, v_: _mask_bd(_mm(k_, v_, TN), RET_V_DIM), k_w, v)
    fill()
    return intra, kv


def _rope(x, cos, sin_signed):
    half = RET_QK_DIM // 2
    width = x.shape[1]
    lane = lax.broadcasted_iota(jnp.int32, x.shape, 1) % RET_QK_DIM
    partner = jnp.where(lane < half, pltpu.roll(x, width - half, 1), pltpu.roll(x, half, 1))
    return x * cos + partner * sin_signed


def _mixer_kernel(x_ref, gain_ref, win_ref, mu_ref, lora_ref, glora_ref, vecs_ref, wout_ref,
                  tri_ref, qones_ref, cos_ref, sin_ref, dd_ref, qw_ref, kw_ref, g64_ref,
                  o_ref, carry_ref, srw_ref, sret_ref, *, seqs, rows):
    @pl.when(pl.program_id(1) == 0)
    def _():
        carry_ref[...] = jnp.zeros_like(carry_ref)
        srw_ref[...] = jnp.zeros_like(srw_ref)
        sret_ref[...] = jnp.zeros_like(sret_ref)

    n_chunks = rows // CHUNK
    w = RWKV_WIDTH
    vecs = vecs_ref[...]
    w0, a0, k_k, k_a, r_k, ln_w, ln_b = (vecs[i:i + 1] for i in range(7))
    qones = qones_ref[...]
    intra_decay = dd_ref[...]
    chunk_decay = g64_ref[...]
    no_fill = _Fill()

    def pieces(z):
        return [z[c * CHUNK:(c + 1) * CHUNK, qd * QUAD:(qd + 1) * QUAD]
                for c in range(n_chunks) for qd in range(N_QUADS)]

    def rows_of(z):
        return [z[c * CHUNK:(c + 1) * CHUNK] for c in range(n_chunks)]

    def front(sq, p):
        pr = p[:, :RWKV_IN]
        first = lax.broadcasted_iota(jnp.int32, pr.shape, 0) == 0
        prev = jnp.where(first, carry_ref[sq], pltpu.roll(pr, 1, 0))
        carry_ref[sq] = pr[rows - 1:rows, :]
        ps = pr + mu_ref[...] * (prev - pr)
        r, k, v = ps[:, :w], ps[:, w:2 * w], ps[:, 2 * w:3 * w]
        wa = ps[:, 3 * w:3 * w + DECAY_LORA + ICL_LORA]
        gd = ps[:, 3 * w + DECAY_LORA + ICL_LORA:]
        is_decay = lax.broadcasted_iota(jnp.int32, wa.shape, 1) < DECAY_LORA
        lora = _dot(jnp.where(is_decay, jnp.tanh(wa), wa).astype(BF16), lora_ref[...])
        log_w = -_softplus(-(w0 + lora[:, :w])) - 0.5
        log_decay = -jnp.exp(log_w)
        a = _sigmoid(a0 + lora[:, w:])
        g = _dot(_sigmoid(gd).astype(BF16), glora_ref[...])
        kk = k * k_k
        kk = kk * lax.rsqrt(jnp.maximum(_head_sum(kk * kk, qones), 1e-24))
        k_mod = k * (1.0 + (a - 1.0) * k_a)
        bonus = _head_sum(r * k_mod * r_k, qones) * v
        cum = _const_mm(tri_ref[...], log_decay, 2)
        tot = jnp.concatenate(
            [jnp.broadcast_to(cum[(i + 1) * CHUNK - 1:(i + 1) * CHUNK], (CHUNK, w))
             for i in range(n_chunks)], axis=0)
        w_inv = jnp.exp(-cum)
        w_end = jnp.exp(tot - cum)
        kka = kk * a
        q0 = RWKV_IN
        cos = cos_ref[...]
        sin = sin_ref[...]
        rq = _rope(p[:, q0:q0 + RET_QK_WIDTH], cos, sin)
        rk = _rope(p[:, q0 + RET_QK_WIDTH:q0 + 2 * RET_QK_WIDTH], cos, sin) * (RET_QK_DIM ** -0.5)
        return dict(
            at=pieces(-kk * jnp.exp(cum - log_decay)), rt=pieces(r * jnp.exp(cum)),
            bt=pieces(kka * w_inv), kt=pieces(k_mod * w_inv),
            bw=pieces(kka * w_end), kw=pieces(k_mod * w_end), v=pieces(v),
            w_tot=[z[:1] for z in pieces(jnp.exp(tot))], bonus=bonus, g=g,
            rq=rows_of(rq), rk=rows_of(rk), rqw=rows_of(rq * qw_ref[...]),
            rkw=rows_of(rk * kw_ref[...]),
            rv=rows_of(p[:, q0 + 2 * RET_QK_WIDTH:q0 + 2 * RET_QK_WIDTH + RET_V_WIDTH]),
            rg=p[:, q0 + 2 * RET_QK_WIDTH + RET_V_WIDTH:])

    def local(f, fill):
        a_hat, u_loc, a_rb, a_rk_v = _rwkv_local(f["at"], f["rt"], f["bt"], f["kt"], f["v"], fill)
        intra, kv = _ret_local(f["rq"], f["rk"], f["rkw"], f["rv"], intra_decay, fill)
        return dict(a_hat=a_hat, u_loc=u_loc, a_rb=a_rb, a_rk_v=a_rk_v, intra=intra, kv=kv)

    def state_stages(sq, f, loc):
        st = dict(rw=[srw_ref[sq * N_QUADS + qd] for qd in range(N_QUADS)], ret=sret_ref[sq])
        res = dict(y=[None] * n_chunks, o=[None] * n_chunks)
        tmp = {}

        def quads(name, c):
            return f[name][c * N_QUADS:(c + 1) * N_QUADS] if name in f else \
                loc[name][c * N_QUADS:(c + 1) * N_QUADS]

        def read_state(c):
            tmp["ps"] = _each(lambda a_, r_, s_: _mm(jnp.concatenate([a_, r_], axis=0), s_, NT),
                              quads("a_hat", c), quads("rt", c), st["rw"])
            res["o"][c] = loc["intra"][c] + _mm(f["rqw"][c], st["ret"])
            st["ret"] = st["ret"] * chunk_decay + loc["kv"][c]

        def update_state(c):
            tmp["u"] = _each(lambda p_, u_: p_[:CHUNK] + u_, tmp["ps"], quads("u_loc", c))
            upd = _each(lambda u_, v_, bw_, kw_: _mm(jnp.concatenate([u_, v_], axis=0),
                                                     jnp.concatenate([bw_, kw_], axis=0), TN),
                        tmp["u"], quads("v", c), quads("bw", c), quads("kw", c))
            st["rw"] = _each(lambda s_, w_, d_: s_ * w_ + _mask_bd(d_, CHUNK),
                             st["rw"], quads("w_tot", c), upd)
            if c == n_chunks - 1:
                for qd in range(N_QUADS):
                    srw_ref[sq * N_QUADS + qd] = st["rw"][qd]
                sret_ref[sq] = st["ret"]

        def emit_y(c, ps, u):
            y = _each(lambda p_, a_, u_, b_: p_[CHUNK:] + b_ + _mm(a_, _bd_rows(u_, CHUNK)),
                      ps, quads("a_rb", c), u, quads("a_rk_v", c))
            res["y"][c] = jnp.concatenate(y, axis=1)

        stages = []
        for c in range(n_chunks):
            stages.append(functools.partial(read_state, c))
            stages.append(functools.partial(update_state, c))
            stages.append(lambda c=c: emit_y(c, tmp["ps"], tmp["u"]))
        return stages, res

    def output_stages(sq, x, f, res):
        tmp = {}
        blocks = []

        def gates():
            y = jnp.concatenate(res["y"], axis=0)
            inv_n = 1.0 / RWKV_HEAD_DIM
            yc = y - _head_sum(y, qones) * inv_n
            var = _head_sum(yc * yc, qones) * inv_n
            y = (yc * lax.rsqrt(var + LN_X_EPS) * ln_w + ln_b + f["bonus"]) * f["g"]
            o = jnp.concatenate(res["o"], axis=0)
            o = jnp.concatenate(
                [o[:, i * RET_V_DIM:(i + 1) * RET_V_DIM]
                 * lax.rsqrt(jnp.mean(jnp.square(o[:, i * RET_V_DIM:(i + 1) * RET_V_DIM]),
                                      axis=-1, keepdims=True) + NORM_EPS)
                 for i in range(RET_HEADS)], axis=1)
            rg = f["rg"]
            o = o * (rg * _sigmoid(rg))
            tmp["mixed"] = jnp.concatenate([y, o], axis=1).astype(BF16)

        def block(j):
            cs = slice(j * MXU_TILE, (j + 1) * MXU_TILE)
            blocks.append(x[:, cs] + _dot(tmp["mixed"], wout_ref[:, cs]))
            if j == D_MODEL // MXU_TILE - 1:
                o_ref[sq] = jnp.concatenate(blocks, axis=1)

        return [gates] + [functools.partial(block, j) for j in range(D_MODEL // MXU_TILE)]

    def proj_stages(h, out):
        def block(j):
            out.append(_dot(h, win_ref[:, j * MXU_TILE:(j + 1) * MXU_TILE]))
        return [functools.partial(block, j) for j in range(PROJ_WIDTH // MXU_TILE)]

    xs = [x_ref[sq] for sq in range(seqs)]
    hs = [_rms_norm(x, gain_ref[...]).astype(BF16) for x in xs]
    p_blocks = []
    _Fill(proj_stages(hs[0], p_blocks)).drain()
    trailing = []
    for sq in range(seqs):
        f = front(sq, jnp.concatenate(p_blocks, axis=1))
        p_blocks = []
        ahead = proj_stages(hs[sq + 1], p_blocks) if sq + 1 < seqs else []
        fill = _Fill(trailing + ahead)
        loc = local(f, fill)
        fill.drain()
        stages, res = state_stages(sq, f, loc)
        trailing = stages + output_stages(sq, xs[sq], f, res)
    _Fill(trailing).drain()


def _ffn_kernel(x_ref, gain_ref, wg_ref, wu_ref, wd_ref, fin_ref, o_ref, *, final_norm):
    x = x_ref[...]
    h = _rms_norm(x, gain_ref[...]).astype(BF16)
    gate = _dot(h, wg_ref[...])
    up = _dot(h, wu_ref[...])
    act = (gate * _sigmoid(gate) * up).astype(BF16)
    y = x + 0.5 * _dot(act, wd_ref[...])
    if final_norm:
        y = _rms_norm(y, fin_ref[...])
    o_ref[...] = y


def _resident(shape):
    return pl.BlockSpec(shape, lambda *_: (0,) * len(shape), pipeline_mode=pl.Buffered(1))


def _ffn(x2, gain, wg, wu, wd, fin, final_norm):
    t = x2.shape[0]
    rows = min(FFN_ROWS, t)
    assert t % rows == 0
    return pl.pallas_call(
        functools.partial(_ffn_kernel, final_norm=final_norm),
        out_shape=jax.ShapeDtypeStruct(x2.shape, F32),
        grid=(t // rows,),
        in_specs=[pl.BlockSpec((rows, D_MODEL), lambda i: (i, 0)),
                  _resident((1, D_MODEL)), _resident((D_MODEL, D_FF)), _resident((D_MODEL, D_FF)),
                  _resident((D_FF, D_MODEL)), _resident((1, D_MODEL))],
        out_specs=pl.BlockSpec((rows, D_MODEL), lambda i: (i, 0)),
        compiler_params=pltpu.CompilerParams(
            dimension_semantics=("arbitrary",), vmem_limit_bytes=VMEM_LIMIT_BYTES),
        name="ffn_final" if final_norm else "ffn",
    )(x2, gain, wg, wu, wd, fin)


def _mixer_tables(seq, rows):
    idx = jnp.arange(rows)
    same_chunk = (idx[:, None] // CHUNK) == (idx[None, :] // CHUNK)
    tri = (same_chunk & (idx[None, :] <= idx[:, None])).astype(BF16)
    hd = jnp.arange(QUAD) // RWKV_HEAD_DIM
    qones = (hd[:, None] == hd[None, :]).astype(BF16)
    half = RET_QK_DIM // 2
    inv_freq = 1.0 / (ROPE_BASE ** jnp.linspace(0.0, 1.0, half, dtype=F32))
    ang = jnp.arange(seq, dtype=F32)[:, None] * inv_freq[None, :]
    cos = jnp.tile(jnp.cos(ang), (1, 2 * RET_HEADS))
    sin = jnp.tile(jnp.concatenate([-jnp.sin(ang), jnp.sin(ang)], axis=1), (1, RET_HEADS))
    log_gamma = jnp.log(1.0 - jnp.power(2.0, -5.0 - jnp.arange(RET_HEADS, dtype=F32)))
    pos = jnp.arange(CHUNK, dtype=F32)
    dist = jnp.abs(pos[:, None] - pos[None, :])
    dd = jnp.exp(log_gamma[:, None, None] * dist)
    dd = jnp.transpose(dd, (1, 0, 2)).reshape(CHUNK, RET_HEADS * CHUNK)
    query_w = jnp.exp(log_gamma[:, None] * (pos + 1.0)[None, :])
    key_w = jnp.exp(log_gamma[:, None] * (CHUNK - 1.0 - pos)[None, :])
    qw = jnp.tile(jnp.repeat(query_w.T, RET_QK_DIM, axis=1), (rows // CHUNK, 1))
    kw = jnp.tile(jnp.repeat(key_w.T, RET_QK_DIM, axis=1), (rows // CHUNK, 1))
    g64 = jnp.repeat(jnp.exp(log_gamma * CHUNK), RET_V_DIM)[None, :]
    return tri, qones, cos, sin, dd, qw, kw, g64


def _mixer(x, gain, w_in, mu, lora, glora, vecs, w_out, tables):
    b, seq, _ = x.shape
    rows = min(MIX_ROWS, seq)
    seqs = MIX_SEQS if b % MIX_SEQS == 0 else 1
    assert seq % rows == 0 and rows % CHUNK == 0
    tri, qones, cos, sin, dd, qw, kw, g64 = tables
    return pl.pallas_call(
        functools.partial(_mixer_kernel, seqs=seqs, rows=rows),
        out_shape=jax.ShapeDtypeStruct(x.shape, F32),
        grid=(b // seqs, seq // rows),
        in_specs=[pl.BlockSpec((seqs, rows, D_MODEL), lambda i, j: (i, j, 0)),
                  _resident((1, D_MODEL)), _resident((D_MODEL, PROJ_WIDTH)),
                  _resident((1, RWKV_IN)), _resident((DECAY_LORA + ICL_LORA, 2 * RWKV_WIDTH)),
                  _resident((GATE_LORA, RWKV_WIDTH)), _resident((8, RWKV_WIDTH)),
                  _resident((D_MODEL, D_MODEL)),
                  _resident((rows, rows)), _resident((QUAD, QUAD)),
                  pl.BlockSpec((rows, RET_QK_WIDTH), lambda i, j: (j, 0)),
                  pl.BlockSpec((rows, RET_QK_WIDTH), lambda i, j: (j, 0)),
                  _resident((CHUNK, RET_QK_WIDTH)), _resident((rows, RET_QK_WIDTH)),
                  _resident((rows, RET_QK_WIDTH)), _resident((1, RET_V_WIDTH))],
        out_specs=pl.BlockSpec((seqs, rows, D_MODEL), lambda i, j: (i, j, 0)),
        scratch_shapes=[pltpu.VMEM((seqs, 1, RWKV_IN), F32),
                        pltpu.VMEM((seqs * N_QUADS, QUAD, QUAD), F32),
                        pltpu.VMEM((seqs, RET_QK_WIDTH, RET_V_WIDTH), F32)],
        compiler_params=pltpu.CompilerParams(
            dimension_semantics=("arbitrary", "arbitrary"), vmem_limit_bytes=VMEM_LIMIT_BYTES),
        name="mixer",
    )(x, gain, w_in, mu, lora, glora, vecs, w_out, tri, qones, cos, sin, dd, qw, kw, g64)


def kernel(x, ffn1_norm, ffn1_w_gate, ffn1_w_up, ffn1_w_down, mix_norm, w_in, shift_mu, w0, w_lora_up, a0, a_lora_up, g_lora_up, k_k, k_a, r_k, ln_x_w, ln_x_b, w_out, ffn2_norm, ffn2_w_gate, ffn2_w_up, ffn2_w_down, final_norm):
    b, seq, d = x.shape
    depth = w_in.shape[0]
    tables = _mixer_tables(seq, min(MIX_ROWS, seq))
    fin = final_norm[None, :]
    zeros = jnp.zeros((DECAY_LORA, RWKV_WIDTH), F32)
    for l in range(depth):
        x = _ffn(x.reshape(b * seq, d), ffn1_norm[l][None, :], ffn1_w_gate[l].astype(BF16),
                 ffn1_w_up[l].astype(BF16), ffn1_w_down[l].astype(BF16), fin, False)
        lora = jnp.concatenate(
            [jnp.concatenate([w_lora_up[l], zeros], axis=1),
             jnp.concatenate([zeros, a_lora_up[l]], axis=1)], axis=0).astype(BF16)
        vecs = jnp.stack([w0[l], a0[l], k_k[l], k_a[l], r_k[l].reshape(-1), ln_x_w[l], ln_x_b[l],
                          jnp.zeros((RWKV_WIDTH,), F32)], axis=0)
        x = _mixer(x.reshape(b, seq, d), mix_norm[l][None, :], w_in[l].astype(BF16),
                   shift_mu[l][None, :], lora, g_lora_up[l].astype(BF16), vecs,
                   w_out[l].astype(BF16), tables)
        x = _ffn(x.reshape(b * seq, d), ffn2_norm[l][None, :], ffn2_w_gate[l].astype(BF16),
                 ffn2_w_up[l].astype(BF16), ffn2_w_down[l].astype(BF16), fin, l == depth - 1)
    return x.reshape(b, seq, d)
```

```python
import functools

import jax
import jax.numpy as jnp
from jax import lax
from jax.experimental import pallas as pl
from jax.experimental.pallas import tpu as pltpu

D_MODEL = 1024
D_FF = 2816
CHUNK = 64
RWKV_HEADS = 8
RWKV_HEAD_DIM = 64
RWKV_WIDTH = RWKV_HEADS * RWKV_HEAD_DIM
DECAY_LORA = 64
ICL_LORA = 64
GATE_LORA = 128
RWKV_IN = 3 * RWKV_WIDTH + DECAY_LORA + ICL_LORA + GATE_LORA
RET_HEADS = 4
RET_QK_DIM = 64
RET_V_DIM = 128
RET_QK_WIDTH = RET_HEADS * RET_QK_DIM
RET_V_WIDTH = RET_HEADS * RET_V_DIM
RET_IN = 2 * RET_QK_WIDTH + 2 * RET_V_WIDTH
PROJ_WIDTH = RWKV_IN + RET_IN
ROPE_BASE = 10000.0
NORM_EPS = 1e-6
LN_X_EPS = 64e-5

GROUP = 4
QUAD = GROUP * RWKV_HEAD_DIM
N_QUADS = RWKV_HEADS // GROUP
MXU_TILE = 256
RET_PAIRS = RET_HEADS // 2
RET_QK_PAIR = 2 * RET_QK_DIM
RET_V_PAIR = 2 * RET_V_DIM
FFN_ROWS = 1024
MIX_ROWS = 256
MIX_SEQS = 4
VMEM_LIMIT_BYTES = 58 * 1024 * 1024

F32 = jnp.float32
BF16 = jnp.bfloat16
NN = (((1,), (0,)), ((), ()))
NT = (((1,), (1,)), ((), ()))
TN = (((0,), (0,)), ((), ()))


def _dot(a, b, dims=NN):
    return lax.dot_general(a, b, dims, preferred_element_type=F32)


def _mm(a, b, dims=NN):
    return _dot(a.astype(BF16), b.astype(BF16), dims)


def _const_mm(c_bf16, x, terms):
    acc = None
    rem = x
    for _ in range(terms):
        piece = rem.astype(BF16)
        rem = rem - piece.astype(F32)
        d = _dot(c_bf16, piece)
        acc = d if acc is None else acc + d
    return acc


def _head_sum(z, quad_ones):
    del quad_ones
    lanes = 2 * RWKV_HEAD_DIM
    low = lax.broadcasted_iota(jnp.int32, (z.shape[0], lanes), 1) < RWKV_HEAD_DIM
    out = []
    for i in range(z.shape[1] // lanes):
        t = z[:, i * lanes:(i + 1) * lanes]
        s_lo = jnp.sum(jnp.where(low, t, 0.0), axis=-1, keepdims=True)
        s_hi = jnp.sum(jnp.where(low, 0.0, t), axis=-1, keepdims=True)
        out.append(jnp.where(low, s_lo, s_hi))
    return jnp.concatenate(out, axis=1)


def _sigmoid(x):
    return 1.0 / (1.0 + jnp.exp(-x))


def _softplus(x):
    return jnp.maximum(x, 0.0) + jnp.log(1.0 + jnp.exp(-jnp.abs(x)))


def _rms_norm(x, gain):
    return x * lax.rsqrt(jnp.mean(x * x, axis=-1, keepdims=True) + NORM_EPS) * gain


def _bd_rows(x, group_lanes):
    lane_head = lax.broadcasted_iota(jnp.int32, x.shape, 1) // group_lanes
    return jnp.concatenate(
        [jnp.where(lane_head == h, x, 0.0) for h in range(x.shape[1] // group_lanes)], axis=0)


def _mask_bd(z, group_lanes):
    lane_head = lax.broadcasted_iota(jnp.int32, (CHUNK, z.shape[1]), 1) // group_lanes
    return jnp.concatenate(
        [jnp.where(lane_head == h, z[h * CHUNK:(h + 1) * CHUNK], 0.0)
         for h in range(z.shape[1] // group_lanes)], axis=0)


class _Fill:
    def __init__(self, pieces=()):
        self.pieces = list(pieces)

    def __call__(self):
        if self.pieces:
            self.pieces.pop(0)()

    def drain(self):
        while self.pieces:
            self.pieces.pop(0)()


def _each(fn, *lists):
    return [fn(*args) for args in zip(*lists)]


def _prod(a_list, b_list, fill):
    out = _each(lambda a, b: _mm(a, _bd_rows(b, CHUNK)), a_list, b_list)
    fill()
    return out


def _add(a_list, b_list):
    return _each(lambda a, b: a + b, a_list, b_list)


def _tri_inverse(ns, fill):
    shape = ns[0].shape
    row = lax.broadcasted_iota(jnp.int32, shape, 0)
    col = lax.broadcasted_iota(jnp.int32, shape, 1) % CHUNK
    same16 = (row // 16) == (col // 16)
    same32 = (row // 32) == (col // 32)
    n0 = [jnp.where(same16, n, 0.0) for n in ns]
    n1 = [jnp.where(same32, jnp.where(same16, 0.0, n), 0.0) for n in ns]
    n2 = [jnp.where(same32, 0.0, n) for n in ns]
    eye = jnp.where(row == col, 1.0, 0.0)
    t = [eye + n for n in n0]
    m = _prod(n0, n0, fill)
    for _ in range(2):
        tm = _prod(_each(lambda a, b: jnp.concatenate([a, b], axis=0), t, m), m, fill)
        t = _each(lambda a, b: a + b[:CHUNK], t, tm)
        m = [x[CHUNK:] for x in tm]
    t = _add(t, _prod(t, m, fill))
    stack = lambda a, b: jnp.concatenate([a, b], axis=0)
    nt = _prod(_each(stack, n1, n2), t, fill)
    x1 = [x[:CHUNK] for x in nt]
    tz = _prod(_each(lambda a, b: stack(a, b[CHUNK:]), t, nt), x1, fill)
    t = _each(lambda a, b: a + b[:CHUNK], t, tz)
    n2t = _each(lambda a, b: a[CHUNK:] + b[CHUNK:], nt, tz)
    return _add(t, _prod(t, n2t, fill))


def _rwkv_local(at, rt, bt, kt, v, fill):
    a = _each(lambda at_, rt_, bt_, kt_: _mm(
        jnp.concatenate([at_, rt_], axis=0),
        jnp.concatenate([_bd_rows(bt_, CHUNK), _bd_rows(kt_, CHUNK)], axis=0), NT),
        at, rt, bt, kt)
    fill()
    row = lax.broadcasted_iota(jnp.int32, (CHUNK, 2 * QUAD), 0)
    col = lax.broadcasted_iota(jnp.int32, (CHUNK, 2 * QUAD), 1) % CHUNK
    a_a = [jnp.where(col < row, x[:CHUNK], 0.0) for x in a]
    a_r = [jnp.where(col <= row, x[CHUNK:], 0.0) for x in a]
    av = _each(lambda x, y, v_: _mm(jnp.concatenate([x[:, QUAD:], y[:, QUAD:]], axis=0),
                                    _bd_rows(v_, CHUNK)), a_a, a_r, v)
    fill()
    t = _tri_inverse([x[:, :QUAD] for x in a_a], fill)
    tu = _each(lambda t_, at_, av_: _mm(
        t_, jnp.concatenate([_bd_rows(at_, CHUNK), _bd_rows(av_[:CHUNK], CHUNK)], axis=1)),
        t, at, av)
    fill()
    return ([x[:, :QUAD] for x in tu], [x[:, QUAD:] for x in tu],
            [x[:, :QUAD] for x in a_r], [x[CHUNK:] for x in av])


def _ret_local(q, k, k_w, v, intra_decay, fill):
    scores = _each(lambda q_, k_, d_: _mm(q_, _bd_rows(k_, RET_QK_DIM), NT) * d_,
                   q, k, intra_decay)
    fill()
    intra = _each(lambda s_, v_: _mm(s_, _bd_rows(v_, RET_V_DIM)), scores, v)
    fill()
    kv = _each(lambda k_, v_: _mask_bd(_mm(k_, v_, TN), RET_V_DIM), k_w, v)
    fill()
    return intra, kv


def _rope(x, cos, sin_signed):
    half = RET_QK_DIM // 2
    width = x.shape[1]
    lane = lax.broadcasted_iota(jnp.int32, x.shape, 1) % RET_QK_DIM
    partner = jnp.where(lane < half, pltpu.roll(x, width - half, 1), pltpu.roll(x, half, 1))
    return x * cos + partner * sin_signed


def _mixer_kernel(x_ref, gain_ref, win_ref, mu_ref, lora_ref, glora_ref, vecs_ref, wout_ref,
                  tri_ref, qones_ref, cos_ref, sin_ref, dd_ref, qw_ref, kw_ref, g64_ref,
                  o_ref, carry_ref, srw_ref, sret_ref, *, seqs, rows):
    @pl.when(pl.program_id(1) == 0)
    def _():
        carry_ref[...] = jnp.zeros_like(carry_ref)
        srw_ref[...] = jnp.zeros_like(srw_ref)
        sret_ref[...] = jnp.zeros_like(sret_ref)

    n_chunks = rows // CHUNK
    w = RWKV_WIDTH
    vecs = vecs_ref[...]
    w0, a0, k_k, k_a, r_k, ln_w, ln_b = (vecs[i:i + 1] for i in range(7))
    qones = qones_ref[...]
    intra_decay = [dd_ref[:, p * RET_QK_PAIR:(p + 1) * RET_QK_PAIR] for p in range(RET_PAIRS)]
    chunk_decay = [g64_ref[:, p * RET_V_PAIR:(p + 1) * RET_V_PAIR] for p in range(RET_PAIRS)]

    def pieces(z):
        return [z[c * CHUNK:(c + 1) * CHUNK, qd * QUAD:(qd + 1) * QUAD]
                for c in range(n_chunks) for qd in range(N_QUADS)]

    def pairs(z, lanes):
        return [z[c * CHUNK:(c + 1) * CHUNK, p * lanes:(p + 1) * lanes]
                for c in range(n_chunks) for p in range(RET_PAIRS)]

    def front(sq, p):
        pr = p[:, :RWKV_IN]
        first = lax.broadcasted_iota(jnp.int32, pr.shape, 0) == 0
        prev = jnp.where(first, carry_ref[sq], pltpu.roll(pr, 1, 0))
        carry_ref[sq] = pr[rows - 1:rows, :]
        ps = pr + mu_ref[...] * (prev - pr)
        r, k, v = ps[:, :w], ps[:, w:2 * w], ps[:, 2 * w:3 * w]
        wa = ps[:, 3 * w:3 * w + DECAY_LORA + ICL_LORA]
        gd = ps[:, 3 * w + DECAY_LORA + ICL_LORA:]
        is_decay = lax.broadcasted_iota(jnp.int32, wa.shape, 1) < DECAY_LORA
        lora = _dot(jnp.where(is_decay, jnp.tanh(wa), wa).astype(BF16), lora_ref[...])
        log_w = -_softplus(-(w0 + lora[:, :w])) - 0.5
        log_decay = -jnp.exp(log_w)
        a = _sigmoid(a0 + lora[:, w:])
        g = _dot(_sigmoid(gd).astype(BF16), glora_ref[...])
        kk = k * k_k
        kk = kk * lax.rsqrt(jnp.maximum(_head_sum(kk * kk, qones), 1e-24))
        k_mod = k * (1.0 + (a - 1.0) * k_a)
        bonus = _head_sum(r * k_mod * r_k, qones) * v
        cum = _const_mm(tri_ref[...], log_decay, 2)
        tot = jnp.concatenate(
            [jnp.broadcast_to(cum[(i + 1) * CHUNK - 1:(i + 1) * CHUNK], (CHUNK, w))
             for i in range(n_chunks)], axis=0)
        w_inv = jnp.exp(-cum)
        w_end = jnp.exp(tot - cum)
        kka = kk * a
        q0 = RWKV_IN
        cos = cos_ref[...]
        sin = sin_ref[...]
        rq = _rope(p[:, q0:q0 + RET_QK_WIDTH], cos, sin)
        rk = _rope(p[:, q0 + RET_QK_WIDTH:q0 + 2 * RET_QK_WIDTH], cos, sin) * (RET_QK_DIM ** -0.5)
        return dict(
            at=pieces(-kk * jnp.exp(cum - log_decay)), rt=pieces(r * jnp.exp(cum)),
            bt=pieces(kka * w_inv), kt=pieces(k_mod * w_inv),
            bw=pieces(kka * w_end), kw=pieces(k_mod * w_end), v=pieces(v),
            w_tot=[z[:1] for z in pieces(jnp.exp(tot))], bonus=bonus, g=g,
            rq=pairs(rq, RET_QK_PAIR), rk=pairs(rk, RET_QK_PAIR),
            rqw=pairs(rq * qw_ref[...], RET_QK_PAIR), rkw=pairs(rk * kw_ref[...], RET_QK_PAIR),
            rv=pairs(p[:, q0 + 2 * RET_QK_WIDTH:q0 + 2 * RET_QK_WIDTH + RET_V_WIDTH], RET_V_PAIR),
            rg=p[:, q0 + 2 * RET_QK_WIDTH + RET_V_WIDTH:])

    def local(f, fill):
        a_hat, u_loc, a_rb, a_rk_v = _rwkv_local(f["at"], f["rt"], f["bt"], f["kt"], f["v"], fill)
        intra, kv = _ret_local(f["rq"], f["rk"], f["rkw"], f["rv"], intra_decay * n_chunks, fill)
        return dict(a_hat=a_hat, u_loc=u_loc, a_rb=a_rb, a_rk_v=a_rk_v, intra=intra, kv=kv)

    def state_stages(sq, f, loc):
        st = dict(rw=[srw_ref[sq * N_QUADS + qd] for qd in range(N_QUADS)],
                  ret=[sret_ref[sq * RET_PAIRS + p] for p in range(RET_PAIRS)])
        res = dict(y=[None] * n_chunks, o=[None] * n_chunks)
        tmp = {}

        def quads(name, c):
            return f[name][c * N_QUADS:(c + 1) * N_QUADS] if name in f else \
                loc[name][c * N_QUADS:(c + 1) * N_QUADS]

        def read_state(c):
            tmp["ps"] = _each(lambda a_, r_, s_: _mm(jnp.concatenate([a_, r_], axis=0), s_, NT),
                              quads("a_hat", c), quads("rt", c), st["rw"])
            ps_ = slice(c * RET_PAIRS, (c + 1) * RET_PAIRS)
            res["o"][c] = jnp.concatenate(
                _each(lambda i_, q_, s_: i_ + _mm(q_, s_), loc["intra"][ps_], f["rqw"][ps_],
                      st["ret"]), axis=1)
            st["ret"] = _each(lambda s_, d_, kv_: s_ * d_ + kv_, st["ret"], chunk_decay,
                              loc["kv"][ps_])

        def update_state(c):
            tmp["u"] = _each(lambda p_, u_: p_[:CHUNK] + u_, tmp["ps"], quads("u_loc", c))
            upd = _each(lambda u_, v_, bw_, kw_: _mm(jnp.concatenate([u_, v_], axis=0),
                                                     jnp.concatenate([bw_, kw_], axis=0), TN),
                        tmp["u"], quads("v", c), quads("bw", c), quads("kw", c))
            st["rw"] = _each(lambda s_, w_, d_: s_ * w_ + _mask_bd(d_, CHUNK),
                             st["rw"], quads("w_tot", c), upd)
            if c == n_chunks - 1:
                for qd in range(N_QUADS):
                    srw_ref[sq * N_QUADS + qd] = st["rw"][qd]
                for p in range(RET_PAIRS):
                    sret_ref[sq * RET_PAIRS + p] = st["ret"][p]

        def emit_y(c, ps, u):
            y = _each(lambda p_, a_, u_, b_: p_[CHUNK:] + b_ + _mm(a_, _bd_rows(u_, CHUNK)),
                      ps, quads("a_rb", c), u, quads("a_rk_v", c))
            res["y"][c] = jnp.concatenate(y, axis=1)

        stages = []
        for c in range(n_chunks):
            stages.append(functools.partial(read_state, c))
            stages.append(functools.partial(update_state, c))
            stages.append(lambda c=c: emit_y(c, tmp["ps"], tmp["u"]))
        return stages, res

    def output_stages(sq, x, f, res):
        tmp = {}
        blocks = []

        def gates():
            y = jnp.concatenate(res["y"], axis=0)
            inv_n = 1.0 / RWKV_HEAD_DIM
            yc = y - _head_sum(y, qones) * inv_n
            var = _head_sum(yc * yc, qones) * inv_n
            y = (yc * lax.rsqrt(var + LN_X_EPS) * ln_w + ln_b + f["bonus"]) * f["g"]
            o = jnp.concatenate(res["o"], axis=0)
            o = jnp.concatenate(
                [o[:, i * RET_V_DIM:(i + 1) * RET_V_DIM]
                 * lax.rsqrt(jnp.mean(jnp.square(o[:, i * RET_V_DIM:(i + 1) * RET_V_DIM]),
                                      axis=-1, keepdims=True) + NORM_EPS)
                 for i in range(RET_HEADS)], axis=1)
            rg = f["rg"]
            o = o * (rg * _sigmoid(rg))
            tmp["mixed"] = jnp.concatenate([y, o], axis=1).astype(BF16)

        def block(j):
            cs = slice(j * MXU_TILE, (j + 1) * MXU_TILE)
            blocks.append(x[:, cs] + _dot(tmp["mixed"], wout_ref[:, cs]))
            if j == D_MODEL // MXU_TILE - 1:
                o_ref[sq] = jnp.concatenate(blocks, axis=1)

        return [gates] + [functools.partial(block, j) for j in range(D_MODEL // MXU_TILE)]

    def proj_stages(h, out):
        def block(j):
            out.append(_dot(h, win_ref[:, j * MXU_TILE:(j + 1) * MXU_TILE]))
        return [functools.partial(block, j) for j in range(PROJ_WIDTH // MXU_TILE)]

    xs = [x_ref[sq] for sq in range(seqs)]
    hs = [_rms_norm(x, gain_ref[...]).astype(BF16) for x in xs]
    def interleave(a, b):
        out = []
        for i in range(max(len(a), len(b))):
            out += a[i:i + 1] + b[i:i + 1]
        return out

    p_blocks = []
    _Fill(proj_stages(hs[0], p_blocks)).drain()
    state_todo, output_todo = [], []
    for sq in range(seqs):
        f = front(sq, jnp.concatenate(p_blocks, axis=1))
        p_blocks = []
        ahead = proj_stages(hs[sq + 1], p_blocks) if sq + 1 < seqs else []
        fill = _Fill(interleave(state_todo, output_todo) + ahead)
        loc = local(f, fill)
        fill.drain()
        stages, res = state_stages(sq, f, loc)
        output_todo = output_stages(sq - 1, xs[sq - 1], *done) if sq else []
        state_todo, done = stages, (f, res)
    _Fill(interleave(state_todo, output_todo)).drain()
    _Fill(output_stages(seqs - 1, xs[seqs - 1], *done)).drain()


def _ffn_kernel(x_ref, gain_ref, wg_ref, wu_ref, wd_ref, fin_ref, o_ref, *, final_norm):
    x = x_ref[...]
    h = _rms_norm(x, gain_ref[...]).astype(BF16)
    gate = _dot(h, wg_ref[...])
    up = _dot(h, wu_ref[...])
    act = (gate * _sigmoid(gate) * up).astype(BF16)
    y = x + 0.5 * _dot(act, wd_ref[...])
    if final_norm:
        y = _rms_norm(y, fin_ref[...])
    o_ref[...] = y


def _resident(shape):
    return pl.BlockSpec(shape, lambda *_: (0,) * len(shape), pipeline_mode=pl.Buffered(1))


def _layer(shape, l):
    return pl.BlockSpec((None,) + shape, lambda *_: (l,) + (0,) * len(shape),
                        pipeline_mode=pl.Buffered(1))


def _ffn(x2, l, gain, wg, wu, wd, fin, final_norm):
    t = x2.shape[0]
    rows = min(FFN_ROWS, t)
    assert t % rows == 0
    return pl.pallas_call(
        functools.partial(_ffn_kernel, final_norm=final_norm),
        out_shape=jax.ShapeDtypeStruct(x2.shape, F32),
        grid=(t // rows,),
        in_specs=[pl.BlockSpec((rows, D_MODEL), lambda i: (i, 0)),
                  _layer((1, D_MODEL), l), _layer((D_MODEL, D_FF), l), _layer((D_MODEL, D_FF), l),
                  _layer((D_FF, D_MODEL), l), _resident((1, D_MODEL))],
        out_specs=pl.BlockSpec((rows, D_MODEL), lambda i: (i, 0)),
        compiler_params=pltpu.CompilerParams(
            dimension_semantics=("arbitrary",), vmem_limit_bytes=VMEM_LIMIT_BYTES),
        name="ffn_final" if final_norm else "ffn",
    )(x2, gain, wg, wu, wd, fin)


def _mixer_tables(seq, rows):
    idx = jnp.arange(rows)
    same_chunk = (idx[:, None] // CHUNK) == (idx[None, :] // CHUNK)
    tri = (same_chunk & (idx[None, :] <= idx[:, None])).astype(BF16)
    hd = jnp.arange(QUAD) // RWKV_HEAD_DIM
    qones = (hd[:, None] == hd[None, :]).astype(BF16)
    half = RET_QK_DIM // 2
    inv_freq = 1.0 / (ROPE_BASE ** jnp.linspace(0.0, 1.0, half, dtype=F32))
    ang = jnp.arange(seq, dtype=F32)[:, None] * inv_freq[None, :]
    cos = jnp.tile(jnp.cos(ang), (1, 2 * RET_HEADS))
    sin = jnp.tile(jnp.concatenate([-jnp.sin(ang), jnp.sin(ang)], axis=1), (1, RET_HEADS))
    log_gamma = jnp.log(1.0 - jnp.power(2.0, -5.0 - jnp.arange(RET_HEADS, dtype=F32)))
    pos = jnp.arange(CHUNK, dtype=F32)
    dist = jnp.abs(pos[:, None] - pos[None, :])
    dd = jnp.exp(log_gamma[:, None, None] * dist)
    dd = jnp.transpose(dd, (1, 0, 2)).reshape(CHUNK, RET_HEADS * CHUNK)
    query_w = jnp.exp(log_gamma[:, None] * (pos + 1.0)[None, :])
    key_w = jnp.exp(log_gamma[:, None] * (CHUNK - 1.0 - pos)[None, :])
    qw = jnp.tile(jnp.repeat(query_w.T, RET_QK_DIM, axis=1), (rows // CHUNK, 1))
    kw = jnp.tile(jnp.repeat(key_w.T, RET_QK_DIM, axis=1), (rows // CHUNK, 1))
    g64 = jnp.repeat(jnp.exp(log_gamma * CHUNK), RET_V_DIM)[None, :]
    return tri, qones, cos, sin, dd, qw, kw, g64


def _mixer(x, l, gain, w_in, mu, lora, glora, vecs, w_out, tables):
    b, seq, _ = x.shape
    rows = min(MIX_ROWS, seq)
    seqs = MIX_SEQS if b % MIX_SEQS == 0 else 1
    assert seq % rows == 0 and rows % CHUNK == 0
    tri, qones, cos, sin, dd, qw, kw, g64 = tables
    return pl.pallas_call(
        functools.partial(_mixer_kernel, seqs=seqs, rows=rows),
        out_shape=jax.ShapeDtypeStruct(x.shape, F32),
        grid=(b // seqs, seq // rows),
        in_specs=[pl.BlockSpec((seqs, rows, D_MODEL), lambda i, j: (i, j, 0)),
                  _layer((1, D_MODEL), l), _layer((D_MODEL, PROJ_WIDTH), l),
                  _layer((1, RWKV_IN), l), _layer((DECAY_LORA + ICL_LORA, 2 * RWKV_WIDTH), l),
                  _layer((GATE_LORA, RWKV_WIDTH), l), _layer((8, RWKV_WIDTH), l),
                  _layer((D_MODEL, D_MODEL), l),
                  _resident((rows, rows)), _resident((QUAD, QUAD)),
                  pl.BlockSpec((rows, RET_QK_WIDTH), lambda i, j: (j, 0)),
                  pl.BlockSpec((rows, RET_QK_WIDTH), lambda i, j: (j, 0)),
                  _resident((CHUNK, RET_QK_WIDTH)), _resident((rows, RET_QK_WIDTH)),
                  _resident((rows, RET_QK_WIDTH)), _resident((1, RET_V_WIDTH))],
        out_specs=pl.BlockSpec((seqs, rows, D_MODEL), lambda i, j: (i, j, 0)),
        scratch_shapes=[pltpu.VMEM((seqs, 1, RWKV_IN), F32),
                        pltpu.VMEM((seqs * N_QUADS, QUAD, QUAD), F32),
                        pltpu.VMEM((seqs * RET_PAIRS, RET_QK_PAIR, RET_V_PAIR), F32)],
        compiler_params=pltpu.CompilerParams(
            dimension_semantics=("arbitrary", "arbitrary"), vmem_limit_bytes=VMEM_LIMIT_BYTES),
        name="mixer",
    )(x, gain, w_in, mu, lora, glora, vecs, w_out, tri, qones, cos, sin, dd, qw, kw, g64)


def kernel(x, ffn1_norm, ffn1_w_gate, ffn1_w_up, ffn1_w_down, mix_norm, w_in, shift_mu, w0, w_lora_up, a0, a_lora_up, g_lora_up, k_k, k_a, r_k, ln_x_w, ln_x_b, w_out, ffn2_norm, ffn2_w_gate, ffn2_w_up, ffn2_w_down, final_norm):
    b, seq, d = x.shape
    depth = w_in.shape[0]
    tables = _mixer_tables(seq, min(MIX_ROWS, seq))
    bf = lambda z: z.astype(BF16)
    row = lambda z: z.reshape(depth, 1, -1)
    ffn1 = (row(ffn1_norm), bf(ffn1_w_gate), bf(ffn1_w_up), bf(ffn1_w_down))
    ffn2 = (row(ffn2_norm), bf(ffn2_w_gate), bf(ffn2_w_up), bf(ffn2_w_down))
    fin = final_norm[None, :]
    zeros = jnp.zeros((depth, DECAY_LORA, RWKV_WIDTH), F32)
    lora = bf(jnp.concatenate(
        [jnp.concatenate([w_lora_up, zeros], axis=2),
         jnp.concatenate([zeros, a_lora_up], axis=2)], axis=1))
    vecs = jnp.stack([w0, a0, k_k, k_a, r_k.reshape(depth, -1), ln_x_w, ln_x_b,
                      jnp.zeros((depth, RWKV_WIDTH), F32)], axis=1)
    mix = (row(mix_norm), bf(w_in), row(shift_mu), lora, bf(g_lora_up), vecs, bf(w_out))
    for l in range(depth):
        x = _ffn(x.reshape(b * seq, d), l, *ffn1, fin, False)
        x = _mixer(x.reshape(b, seq, d), l, *mix, tables)
        x = _ffn(x.reshape(b * seq, d), l, *ffn2, fin, l == depth - 1)
    return x.reshape(b, seq, d)
```

```python
import functools

import jax
import jax.numpy as jnp
from jax import lax
from jax.experimental import pallas as pl
from jax.experimental.pallas import tpu as pltpu

D_MODEL = 1024
D_FF = 2816
CHUNK = 64
RWKV_HEADS = 8
RWKV_HEAD_DIM = 64
RWKV_WIDTH = RWKV_HEADS * RWKV_HEAD_DIM
DECAY_LORA = 64
ICL_LORA = 64
GATE_LORA = 128
RWKV_IN = 3 * RWKV_WIDTH + DECAY_LORA + ICL_LORA + GATE_LORA
RET_HEADS = 4
RET_QK_DIM = 64
RET_V_DIM = 128
RET_QK_WIDTH = RET_HEADS * RET_QK_DIM
RET_V_WIDTH = RET_HEADS * RET_V_DIM
RET_IN = 2 * RET_QK_WIDTH + 2 * RET_V_WIDTH
PROJ_WIDTH = RWKV_IN + RET_IN
ROPE_BASE = 10000.0
NORM_EPS = 1e-6
LN_X_EPS = 64e-5

GROUP = 4
QUAD = GROUP * RWKV_HEAD_DIM
N_QUADS = RWKV_HEADS // GROUP
MXU_TILE = 256
RET_PAIRS = RET_HEADS // 2
RET_QK_PAIR = 2 * RET_QK_DIM
RET_V_PAIR = 2 * RET_V_DIM
FFN_ROWS = 1024
MIX_ROWS = 256
MIX_SEQS = 4
VMEM_LIMIT_BYTES = 58 * 1024 * 1024

F32 = jnp.float32
BF16 = jnp.bfloat16
NN = (((1,), (0,)), ((), ()))
NT = (((1,), (1,)), ((), ()))
TN = (((0,), (0,)), ((), ()))


def _dot(a, b, dims=NN):
    return lax.dot_general(a, b, dims, preferred_element_type=F32)


def _mm(a, b, dims=NN):
    return _dot(a.astype(BF16), b.astype(BF16), dims)


def _const_mm(c_bf16, x, terms):
    acc = None
    rem = x
    for _ in range(terms):
        piece = rem.astype(BF16)
        rem = rem - piece.astype(F32)
        d = _dot(c_bf16, piece)
        acc = d if acc is None else acc + d
    return acc


def _head_sum(z, quad_ones):
    del quad_ones
    lanes = 2 * RWKV_HEAD_DIM
    low = lax.broadcasted_iota(jnp.int32, (z.shape[0], lanes), 1) < RWKV_HEAD_DIM
    out = []
    for i in range(z.shape[1] // lanes):
        t = z[:, i * lanes:(i + 1) * lanes]
        s_lo = jnp.sum(jnp.where(low, t, 0.0), axis=-1, keepdims=True)
        s_hi = jnp.sum(jnp.where(low, 0.0, t), axis=-1, keepdims=True)
        out.append(jnp.where(low, s_lo, s_hi))
    return jnp.concatenate(out, axis=1)


def _sigmoid(x):
    return 1.0 / (1.0 + jnp.exp(-x))


def _softplus(x):
    return jnp.maximum(x, 0.0) + jnp.log(1.0 + jnp.exp(-jnp.abs(x)))


def _rms_norm(x, gain):
    return x * lax.rsqrt(jnp.mean(x * x, axis=-1, keepdims=True) + NORM_EPS) * gain


def _bd_rows(x, group_lanes):
    lane_head = lax.broadcasted_iota(jnp.int32, x.shape, 1) // group_lanes
    return jnp.concatenate(
        [jnp.where(lane_head == h, x, 0.0) for h in range(x.shape[1] // group_lanes)], axis=0)


def _mask_bd(z, group_lanes):
    lane_head = lax.broadcasted_iota(jnp.int32, (CHUNK, z.shape[1]), 1) // group_lanes
    return jnp.concatenate(
        [jnp.where(lane_head == h, z[h * CHUNK:(h + 1) * CHUNK], 0.0)
         for h in range(z.shape[1] // group_lanes)], axis=0)


def _chunk_rhs(z, n_heads):
    rows, width = z.shape
    n_chunks = rows // CHUNK
    zt = z.T
    rolled = [zt] + [pltpu.roll(zt, j * CHUNK, 1) for j in range(1, n_chunks)]
    lane_block = lax.broadcasted_iota(jnp.int32, (CHUNK, rows), 1) // CHUNK
    out = []
    for c in range(n_chunks):
        blocks = [jnp.where(lane_block == h,
                            rolled[(h - c) % n_chunks][h * CHUNK:(h + 1) * CHUNK], 0.0)[:, :width]
                  for h in range(n_heads)]
        out.append(jnp.concatenate(blocks, axis=0))
    return out


class _Fill:
    def __init__(self, pieces=(), calls=1):
        self.pieces = list(pieces)
        self.calls = calls

    def __call__(self):
        share = -(-len(self.pieces) // max(self.calls, 1))
        self.calls -= 1
        for _ in range(share):
            self.pieces.pop(0)()

    def drain(self):
        while self.pieces:
            self.pieces.pop(0)()


LOCAL_STAGES = 13


def _each(fn, *lists):
    return [fn(*args) for args in zip(*lists)]


def _prod(a_list, b_list, fill):
    out = _each(lambda a, b: _mm(a, _bd_rows(b, CHUNK)), a_list, b_list)
    fill()
    return out


def _add(a_list, b_list):
    return _each(lambda a, b: a + b, a_list, b_list)


def _tri_inverse(ns, fill):
    shape = ns[0].shape
    row = lax.broadcasted_iota(jnp.int32, shape, 0)
    col = lax.broadcasted_iota(jnp.int32, shape, 1) % CHUNK
    same16 = (row // 16) == (col // 16)
    same32 = (row // 32) == (col // 32)
    n0 = [jnp.where(same16, n, 0.0) for n in ns]
    n1 = [jnp.where(same32, jnp.where(same16, 0.0, n), 0.0) for n in ns]
    n2 = [jnp.where(same32, 0.0, n) for n in ns]
    eye = jnp.where(row == col, 1.0, 0.0)
    t = [eye + n for n in n0]
    m = _prod(n0, n0, fill)
    for _ in range(2):
        tm = _prod(_each(lambda a, b: jnp.concatenate([a, b], axis=0), t, m), m, fill)
        t = _each(lambda a, b: a + b[:CHUNK], t, tm)
        m = [x[CHUNK:] for x in tm]
    t = _add(t, _prod(t, m, fill))
    stack = lambda a, b: jnp.concatenate([a, b], axis=0)
    nt = _prod(_each(stack, n1, n2), t, fill)
    x1 = [x[:CHUNK] for x in nt]
    tz = _prod(_each(lambda a, b: stack(a, b[CHUNK:]), t, nt), x1, fill)
    t = _each(lambda a, b: a + b[:CHUNK], t, tz)
    n2t = _each(lambda a, b: a[CHUNK:] + b[CHUNK:], nt, tz)
    return _add(t, _prod(t, n2t, fill))


def _rwkv_local(at, rt, wb, wk, v, fill):
    a = _each(lambda at_, rt_, wb_, wk_: _mm(
        jnp.concatenate([at_, rt_], axis=0), jnp.concatenate([wb_, wk_], axis=1)),
        at, rt, wb, wk)
    fill()
    row = lax.broadcasted_iota(jnp.int32, (CHUNK, 2 * QUAD), 0)
    col = lax.broadcasted_iota(jnp.int32, (CHUNK, 2 * QUAD), 1) % CHUNK
    a_a = [jnp.where(col < row, x[:CHUNK], 0.0) for x in a]
    a_r = [jnp.where(col <= row, x[CHUNK:], 0.0) for x in a]
    av = _each(lambda x, y, v_: _mm(jnp.concatenate([x[:, QUAD:], y[:, QUAD:]], axis=0),
                                    _bd_rows(v_, CHUNK)), a_a, a_r, v)
    fill()
    t = _tri_inverse([x[:, :QUAD] for x in a_a], fill)
    tu = _each(lambda t_, at_, av_: _mm(
        t_, jnp.concatenate([_bd_rows(at_, CHUNK),
                             _bd_rows(av_[:CHUNK].astype(BF16), CHUNK)], axis=1)),
        t, at, av)
    fill()
    return ([x[:, :QUAD] for x in tu], [x[:, QUAD:] for x in tu],
            [x[:, :QUAD] for x in a_r], [x[CHUNK:] for x in av])


def _ret_local(q, k, k_w, v, intra_decay, fill):
    scores = _each(lambda q_, k_, d_: _mm(q_, k_) * d_, q, k, intra_decay)
    fill()
    intra = _each(lambda s_, v_: _mm(s_, _bd_rows(v_, RET_V_DIM)), scores, v)
    fill()
    kv = _each(lambda k_, v_: _mask_bd(_mm(k_, v_, TN), RET_V_DIM), k_w, v)
    fill()
    return intra, kv


def _rope(x, cos, sin_signed):
    half = RET_QK_DIM // 2
    width = x.shape[1]
    lane = lax.broadcasted_iota(jnp.int32, x.shape, 1) % RET_QK_DIM
    partner = jnp.where(lane < half, pltpu.roll(x, width - half, 1), pltpu.roll(x, half, 1))
    return x * cos + partner * sin_signed


def _mixer_kernel(x_ref, gain_ref, win_ref, mu_ref, lora_ref, glora_ref, vecs_ref, wout_ref,
                  tri_ref, qones_ref, cos_ref, sin_ref, dd_ref, qw_ref, kw_ref, g64_ref,
                  o_ref, carry_ref, srw_ref, sret_ref, *, seqs, rows):
    @pl.when(pl.program_id(1) == 0)
    def _():
        carry_ref[...] = jnp.zeros_like(carry_ref)
        srw_ref[...] = jnp.zeros_like(srw_ref)
        sret_ref[...] = jnp.zeros_like(sret_ref)

    n_chunks = rows // CHUNK
    w = RWKV_WIDTH
    vecs = vecs_ref[...]
    w0, a0, k_k, k_a, r_k, ln_w, ln_b = (vecs[i:i + 1] for i in range(7))
    qones = qones_ref[...]
    intra_decay = [dd_ref[:, p * RET_QK_PAIR:(p + 1) * RET_QK_PAIR] for p in range(RET_PAIRS)]
    chunk_decay = [g64_ref[:, p * RET_V_PAIR:(p + 1) * RET_V_PAIR] for p in range(RET_PAIRS)]

    def pieces(z):
        return [z[c * CHUNK:(c + 1) * CHUNK, qd * QUAD:(qd + 1) * QUAD]
                for c in range(n_chunks) for qd in range(N_QUADS)]

    def pairs(z, lanes):
        return [z[c * CHUNK:(c + 1) * CHUNK, p * lanes:(p + 1) * lanes]
                for c in range(n_chunks) for p in range(RET_PAIRS)]

    def front_stages(sq, p_blocks):
        t, f = {}, {}
        q0 = RWKV_IN

        def shift():
            p = t["p"] = jnp.concatenate(p_blocks, axis=1)
            pr = p[:, :RWKV_IN]
            first = lax.broadcasted_iota(jnp.int32, pr.shape, 0) == 0
            prev = jnp.where(first, carry_ref[sq], pltpu.roll(pr, 1, 0))
            carry_ref[sq] = pr[rows - 1:rows, :]
            t["ps"] = pr + mu_ref[...] * (prev - pr)

        def loras():
            ps = t["ps"]
            wa = ps[:, 3 * w:3 * w + DECAY_LORA + ICL_LORA]
            gd = ps[:, 3 * w + DECAY_LORA + ICL_LORA:]
            is_decay = lax.broadcasted_iota(jnp.int32, wa.shape, 1) < DECAY_LORA
            lora = _dot(jnp.where(is_decay, jnp.tanh(wa), wa).astype(BF16), lora_ref[...])
            log_w = -_softplus(-(w0 + lora[:, :w])) - 0.5
            t["log_decay"] = -jnp.exp(log_w)
            t["a"] = _sigmoid(a0 + lora[:, w:])
            f["g"] = _dot(_sigmoid(gd).astype(BF16), glora_ref[...])

        def keys():
            ps = t["ps"]
            r, k, v = ps[:, :w], ps[:, w:2 * w], ps[:, 2 * w:3 * w]
            kk = k * k_k
            t["kk"] = kk * lax.rsqrt(jnp.maximum(_head_sum(kk * kk, qones), 1e-24))
            t["k_mod"] = k * (1.0 + (t["a"] - 1.0) * k_a)
            f["bonus"] = _head_sum(r * t["k_mod"] * r_k, qones) * v
            f["v"] = pieces(v.astype(BF16))

        def decays():
            cum = _const_mm(tri_ref[...], t["log_decay"], 2)
            tot = jnp.concatenate(
                [jnp.broadcast_to(cum[(i + 1) * CHUNK - 1:(i + 1) * CHUNK], (CHUNK, w))
                 for i in range(n_chunks)], axis=0)
            t["cum"], t["tot"] = cum, tot
            tot_t = [jnp.exp(tot[:, qd * QUAD:(qd + 1) * QUAD]).T for qd in range(N_QUADS)]
            f["w_tot"] = [tot_t[qd][:, c * CHUNK:c * CHUNK + 1]
                          for c in range(n_chunks) for qd in range(N_QUADS)]
            f["at"] = pieces((-t["kk"] * jnp.exp(cum - t["log_decay"])).astype(BF16))
            f["rt"] = pieces((t["ps"][:, :w] * jnp.exp(cum)).astype(BF16))

        def scaled():
            kka = t["kk"] * t["a"]
            w_inv = jnp.exp(-t["cum"])
            w_end = jnp.exp(t["tot"] - t["cum"])
            scale = lambda z, s: pieces((z * s).astype(BF16))
            f["bw"], f["kw"] = scale(kka, w_end), scale(t["k_mod"], w_end)

            def rhs(z):
                per_quad = [_chunk_rhs(z[:, qd * QUAD:(qd + 1) * QUAD], GROUP)
                            for qd in range(N_QUADS)]
                return [per_quad[qd][c] for c in range(n_chunks) for qd in range(N_QUADS)]
            f["wb"], f["wk"] = rhs(kka * w_inv), rhs(t["k_mod"] * w_inv)

        def retention():
            p = t["p"]
            cos = cos_ref[...]
            sin = sin_ref[...]
            rq = _rope(p[:, q0:q0 + RET_QK_WIDTH], cos, sin)
            rk = _rope(p[:, q0 + RET_QK_WIDTH:q0 + 2 * RET_QK_WIDTH], cos, sin) \
                * (RET_QK_DIM ** -0.5)
            qk = lambda z: pairs(z.astype(BF16), RET_QK_PAIR)
            f["rq"] = qk(rq)
            per_pair = [_chunk_rhs(rk[:, p_ * RET_QK_PAIR:(p_ + 1) * RET_QK_PAIR], 2)
                        for p_ in range(RET_PAIRS)]
            f["rk"] = [per_pair[p_][c] for c in range(n_chunks) for p_ in range(RET_PAIRS)]
            f["rqw"], f["rkw"] = qk(rq * qw_ref[...]), qk(rk * kw_ref[...])
            f["rv"] = pairs(p[:, q0 + 2 * RET_QK_WIDTH:q0 + 2 * RET_QK_WIDTH + RET_V_WIDTH]
                            .astype(BF16), RET_V_PAIR)
            f["rg"] = p[:, q0 + 2 * RET_QK_WIDTH + RET_V_WIDTH:]

        return [shift, loras, keys, decays, scaled, retention], f

    def local(f, fill):
        a_hat, u_loc, a_rb, a_rk_v = _rwkv_local(f["at"], f["rt"], f["wb"], f["wk"], f["v"], fill)
        intra, kv = _ret_local(f["rq"], f["rk"], f["rkw"], f["rv"], intra_decay * n_chunks, fill)
        return dict(a_hat=a_hat, u_loc=u_loc, a_rb=a_rb, a_rk_v=a_rk_v, intra=intra, kv=kv)

    def state_stages(sq, f, loc):
        st = dict(rw=[srw_ref[sq * N_QUADS + qd] for qd in range(N_QUADS)],
                  ret=[sret_ref[sq * RET_PAIRS + p] for p in range(RET_PAIRS)])
        res = dict(y=[None] * n_chunks, o=[None] * n_chunks)
        tmp = {}

        def quads(name, c):
            return f[name][c * N_QUADS:(c + 1) * N_QUADS] if name in f else \
                loc[name][c * N_QUADS:(c + 1) * N_QUADS]

        def read_state(c):
            tmp["ps"] = _each(lambda a_, r_, s_: _mm(
                jnp.concatenate([a_.astype(BF16), r_], axis=0), s_),
                              quads("a_hat", c), quads("rt", c), st["rw"])
            ps_ = slice(c * RET_PAIRS, (c + 1) * RET_PAIRS)
            res["o"][c] = jnp.concatenate(
                _each(lambda i_, q_, s_: i_ + _mm(q_, s_), loc["intra"][ps_], f["rqw"][ps_],
                      st["ret"]), axis=1)
            st["ret"] = _each(lambda s_, d_, kv_: s_ * d_ + kv_, st["ret"], chunk_decay,
                              loc["kv"][ps_])

        def update_state(c):
            tmp["u"] = _each(lambda p_, u_: p_[:CHUNK] + u_, tmp["ps"], quads("u_loc", c))
            upd = _each(lambda u_, v_, bw_, kw_: _mm(jnp.concatenate([bw_, kw_], axis=0),
                                                     jnp.concatenate([u_.astype(BF16), v_], axis=0),
                                                     TN),
                        tmp["u"], quads("v", c), quads("bw", c), quads("kw", c))
            st["rw"] = _each(lambda s_, w_, d_: s_ * w_ + _mask_bd(d_, CHUNK),
                             st["rw"], quads("w_tot", c), upd)
            if c == n_chunks - 1:
                for qd in range(N_QUADS):
                    srw_ref[sq * N_QUADS + qd] = st["rw"][qd]
                for p in range(RET_PAIRS):
                    sret_ref[sq * RET_PAIRS + p] = st["ret"][p]

        def emit_y(c, ps, u):
            y = _each(lambda p_, a_, u_, b_: p_[CHUNK:] + b_ + _mm(a_, _bd_rows(u_, CHUNK)),
                      ps, quads("a_rb", c), u, quads("a_rk_v", c))
            res["y"][c] = jnp.concatenate(y, axis=1)

        stages = []
        for c in range(n_chunks):
            stages.append(functools.partial(read_state, c))
            stages.append(functools.partial(update_state, c))
            stages.append(lambda c=c: emit_y(c, tmp["ps"], tmp["u"]))
        return stages, res

    def output_stages(sq, x, f, res):
        tmp = {}
        blocks = []

        def gates():
            y = jnp.concatenate(res["y"], axis=0)
            inv_n = 1.0 / RWKV_HEAD_DIM
            yc = y - _head_sum(y, qones) * inv_n
            var = _head_sum(yc * yc, qones) * inv_n
            y = (yc * lax.rsqrt(var + LN_X_EPS) * ln_w + ln_b + f["bonus"]) * f["g"]
            o = jnp.concatenate(res["o"], axis=0)
            o = jnp.concatenate(
                [o[:, i * RET_V_DIM:(i + 1) * RET_V_DIM]
                 * lax.rsqrt(jnp.mean(jnp.square(o[:, i * RET_V_DIM:(i + 1) * RET_V_DIM]),
                                      axis=-1, keepdims=True) + NORM_EPS)
                 for i in range(RET_HEADS)], axis=1)
            rg = f["rg"]
            o = o * (rg * _sigmoid(rg))
            tmp["mixed"] = jnp.concatenate([y, o], axis=1).astype(BF16)

        def block(j):
            cs = slice(j * MXU_TILE, (j + 1) * MXU_TILE)
            blocks.append(x[:, cs] + _dot(tmp["mixed"], wout_ref[:, cs]))
            if j == D_MODEL // MXU_TILE - 1:
                o_ref[sq] = jnp.concatenate(blocks, axis=1)

        return [gates] + [functools.partial(block, j) for j in range(D_MODEL // MXU_TILE)]

    def proj_stages(h, out):
        def block(j):
            out.append(_dot(h, win_ref[:, j * MXU_TILE:(j + 1) * MXU_TILE]))
        return [functools.partial(block, j) for j in range(PROJ_WIDTH // MXU_TILE)]

    xs = [x_ref[sq] for sq in range(seqs)]
    hs = [_rms_norm(x, gain_ref[...]).astype(BF16) for x in xs]
    def interleave(a, b):
        out = []
        for i in range(max(len(a), len(b))):
            out += a[i:i + 1] + b[i:i + 1]
        return out

    p_blocks = [[] for _ in range(seqs)]
    fronts = [front_stages(sq, p_blocks[sq]) for sq in range(seqs)]
    _Fill(proj_stages(hs[0], p_blocks[0])).drain()
    _Fill(interleave(fronts[0][0], proj_stages(hs[1], p_blocks[1]) if seqs > 1 else [])).drain()
    state_todo, output_todo = [], []
    for sq in range(seqs):
        f = fronts[sq][1]
        ahead = fronts[sq + 1][0] if sq + 1 < seqs else []
        ahead = interleave(ahead, proj_stages(hs[sq + 2], p_blocks[sq + 2]) if sq + 2 < seqs else [])
        fill = _Fill(interleave(interleave(state_todo, output_todo), ahead), LOCAL_STAGES)
        loc = local(f, fill)
        fill.drain()
        stages, res = state_stages(sq, f, loc)
        output_todo = output_stages(sq - 1, xs[sq - 1], *done) if sq else []
        state_todo, done = stages, (f, res)
    _Fill(interleave(state_todo, output_todo)).drain()
    _Fill(output_stages(seqs - 1, xs[seqs - 1], *done)).drain()


def _ffn_kernel(x_ref, gain_ref, wg_ref, wu_ref, wd_ref, fin_ref, o_ref, *, final_norm):
    x = x_ref[...]
    h = _rms_norm(x, gain_ref[...]).astype(BF16)
    gate = _dot(h, wg_ref[...])
    up = _dot(h, wu_ref[...])
    act = (gate * _sigmoid(gate) * up).astype(BF16)
    y = x + 0.5 * _dot(act, wd_ref[...])
    if final_norm:
        y = _rms_norm(y, fin_ref[...])
    o_ref[...] = y


def _resident(shape):
    return pl.BlockSpec(shape, lambda *_: (0,) * len(shape), pipeline_mode=pl.Buffered(1))


def _layer(shape, l):
    return pl.BlockSpec((None,) + shape, lambda *_: (l,) + (0,) * len(shape),
                        pipeline_mode=pl.Buffered(1))


def _ffn(x2, l, gain, wg, wu, wd, fin, final_norm):
    t = x2.shape[0]
    rows = min(FFN_ROWS, t)
    assert t % rows == 0
    return pl.pallas_call(
        functools.partial(_ffn_kernel, final_norm=final_norm),
        out_shape=jax.ShapeDtypeStruct(x2.shape, F32),
        grid=(t // rows,),
        in_specs=[pl.BlockSpec((rows, D_MODEL), lambda i: (i, 0)),
                  _layer((1, D_MODEL), l), _layer((D_MODEL, D_FF), l), _layer((D_MODEL, D_FF), l),
                  _layer((D_FF, D_MODEL), l), _resident((1, D_MODEL))],
        out_specs=pl.BlockSpec((rows, D_MODEL), lambda i: (i, 0)),
        compiler_params=pltpu.CompilerParams(
            dimension_semantics=("arbitrary",), vmem_limit_bytes=VMEM_LIMIT_BYTES),
        name="ffn_final" if final_norm else "ffn",
    )(x2, gain, wg, wu, wd, fin)


def _mixer_tables(seq, rows):
    idx = jnp.arange(rows)
    same_chunk = (idx[:, None] // CHUNK) == (idx[None, :] // CHUNK)
    tri = (same_chunk & (idx[None, :] <= idx[:, None])).astype(BF16)
    hd = jnp.arange(QUAD) // RWKV_HEAD_DIM
    qones = (hd[:, None] == hd[None, :]).astype(BF16)
    half = RET_QK_DIM // 2
    inv_freq = 1.0 / (ROPE_BASE ** jnp.linspace(0.0, 1.0, half, dtype=F32))
    ang = jnp.arange(seq, dtype=F32)[:, None] * inv_freq[None, :]
    cos = jnp.tile(jnp.cos(ang), (1, 2 * RET_HEADS))
    sin = jnp.tile(jnp.concatenate([-jnp.sin(ang), jnp.sin(ang)], axis=1), (1, RET_HEADS))
    log_gamma = jnp.log(1.0 - jnp.power(2.0, -5.0 - jnp.arange(RET_HEADS, dtype=F32)))
    pos = jnp.arange(CHUNK, dtype=F32)
    dist = jnp.abs(pos[:, None] - pos[None, :])
    dd = jnp.exp(log_gamma[:, None, None] * dist)
    dd = jnp.transpose(dd, (1, 0, 2)).reshape(CHUNK, RET_HEADS * CHUNK)
    query_w = jnp.exp(log_gamma[:, None] * (pos + 1.0)[None, :])
    key_w = jnp.exp(log_gamma[:, None] * (CHUNK - 1.0 - pos)[None, :])
    qw = jnp.tile(jnp.repeat(query_w.T, RET_QK_DIM, axis=1), (rows // CHUNK, 1))
    kw = jnp.tile(jnp.repeat(key_w.T, RET_QK_DIM, axis=1), (rows // CHUNK, 1))
    g64 = jnp.repeat(jnp.exp(log_gamma * CHUNK), RET_V_DIM)[None, :]
    return tri, qones, cos, sin, dd, qw, kw, g64


def _mixer(x, l, gain, w_in, mu, lora, glora, vecs, w_out, tables):
    b, seq, _ = x.shape
    rows = min(MIX_ROWS, seq)
    seqs = MIX_SEQS if b % MIX_SEQS == 0 else 1
    assert seq % rows == 0 and rows % CHUNK == 0
    tri, qones, cos, sin, dd, qw, kw, g64 = tables
    return pl.pallas_call(
        functools.partial(_mixer_kernel, seqs=seqs, rows=rows),
        out_shape=jax.ShapeDtypeStruct(x.shape, F32),
        grid=(b // seqs, seq // rows),
        in_specs=[pl.BlockSpec((seqs, rows, D_MODEL), lambda i, j: (i, j, 0)),
                  _layer((1, D_MODEL), l), _layer((D_MODEL, PROJ_WIDTH), l),
                  _layer((1, RWKV_IN), l), _layer((DECAY_LORA + ICL_LORA, 2 * RWKV_WIDTH), l),
                  _layer((GATE_LORA, RWKV_WIDTH), l), _layer((8, RWKV_WIDTH), l),
                  _layer((D_MODEL, D_MODEL), l),
                  _resident((rows, rows)), _resident((QUAD, QUAD)),
                  pl.BlockSpec((rows, RET_QK_WIDTH), lambda i, j: (j, 0)),
                  pl.BlockSpec((rows, RET_QK_WIDTH), lambda i, j: (j, 0)),
                  _resident((CHUNK, RET_QK_WIDTH)), _resident((rows, RET_QK_WIDTH)),
                  _resident((rows, RET_QK_WIDTH)), _resident((1, RET_V_WIDTH))],
        out_specs=pl.BlockSpec((seqs, rows, D_MODEL), lambda i, j: (i, j, 0)),
        scratch_shapes=[pltpu.VMEM((seqs, 1, RWKV_IN), F32),
                        pltpu.VMEM((seqs * N_QUADS, QUAD, QUAD), F32),
                        pltpu.VMEM((seqs * RET_PAIRS, RET_QK_PAIR, RET_V_PAIR), F32)],
        compiler_params=pltpu.CompilerParams(
            dimension_semantics=("arbitrary", "arbitrary"), vmem_limit_bytes=VMEM_LIMIT_BYTES),
        name="mixer",
    )(x, gain, w_in, mu, lora, glora, vecs, w_out, tri, qones, cos, sin, dd, qw, kw, g64)


def kernel(x, ffn1_norm, ffn1_w_gate, ffn1_w_up, ffn1_w_down, mix_norm, w_in, shift_mu, w0, w_lora_up, a0, a_lora_up, g_lora_up, k_k, k_a, r_k, ln_x_w, ln_x_b, w_out, ffn2_norm, ffn2_w_gate, ffn2_w_up, ffn2_w_down, final_norm):
    b, seq, d = x.shape
    depth = w_in.shape[0]
    tables = _mixer_tables(seq, min(MIX_ROWS, seq))
    bf = lambda z: z.astype(BF16)
    row = lambda z: z.reshape(depth, 1, -1)
    ffn1 = (row(ffn1_norm), bf(ffn1_w_gate), bf(ffn1_w_up), bf(ffn1_w_down))
    ffn2 = (row(ffn2_norm), bf(ffn2_w_gate), bf(ffn2_w_up), bf(ffn2_w_down))
    fin = final_norm[None, :]
    zeros = jnp.zeros((depth, DECAY_LORA, RWKV_WIDTH), F32)
    lora = bf(jnp.concatenate(
        [jnp.concatenate([w_lora_up, zeros], axis=2),
         jnp.concatenate([zeros, a_lora_up], axis=2)], axis=1))
    vecs = jnp.stack([w0, a0, k_k, k_a, r_k.reshape(depth, -1), ln_x_w, ln_x_b,
                      jnp.zeros((depth, RWKV_WIDTH), F32)], axis=1)
    mix = (row(mix_norm), bf(w_in), row(shift_mu), lora, bf(g_lora_up), vecs, bf(w_out))
    for l in range(depth):
        x = _ffn(x.reshape(b * seq, d), l, *ffn1, fin, False)
        x = _mixer(x.reshape(b, seq, d), l, *mix, tables)
        x = _ffn(x.reshape(b * seq, d), l, *ffn2, fin, l == depth - 1)
    return x.reshape(b, seq, d)
```

```python
import functools

import jax
import jax.numpy as jnp
from jax import lax
from jax.experimental import pallas as pl
from jax.experimental.pallas import tpu as pltpu

D_MODEL = 1024
D_FF = 2816
CHUNK = 64
RWKV_HEADS = 8
RWKV_HEAD_DIM = 64
RWKV_WIDTH = RWKV_HEADS * RWKV_HEAD_DIM
DECAY_LORA = 64
ICL_LORA = 64
GATE_LORA = 128
RWKV_IN = 3 * RWKV_WIDTH + DECAY_LORA + ICL_LORA + GATE_LORA
RET_HEADS = 4
RET_QK_DIM = 64
RET_V_DIM = 128
RET_QK_WIDTH = RET_HEADS * RET_QK_DIM
RET_V_WIDTH = RET_HEADS * RET_V_DIM
RET_IN = 2 * RET_QK_WIDTH + 2 * RET_V_WIDTH
PROJ_WIDTH = RWKV_IN + RET_IN
ROPE_BASE = 10000.0
NORM_EPS = 1e-6
LN_X_EPS = 64e-5

GROUP = 4
QUAD = GROUP * RWKV_HEAD_DIM
N_QUADS = RWKV_HEADS // GROUP
MXU_TILE = 256
RET_PAIRS = RET_HEADS // 2
RET_QK_PAIR = 2 * RET_QK_DIM
RET_V_PAIR = 2 * RET_V_DIM
FFN_ROWS = 1024
MIX_ROWS = 256
MIX_SEQS = 4
VMEM_LIMIT_BYTES = 58 * 1024 * 1024

F32 = jnp.float32
BF16 = jnp.bfloat16
NN = (((1,), (0,)), ((), ()))
TN = (((0,), (0,)), ((), ()))


def _dot(a, b, dims=NN):
    return lax.dot_general(a, b, dims, preferred_element_type=F32)


def _mm(a, b, dims=NN):
    return _dot(a.astype(BF16), b.astype(BF16), dims)


def _const_mm(c_bf16, x, terms):
    acc = None
    rem = x
    for _ in range(terms):
        piece = rem.astype(BF16)
        rem = rem - piece.astype(F32)
        d = _dot(c_bf16, piece)
        acc = d if acc is None else acc + d
    return acc


def _head_sum(z):
    lanes = 2 * RWKV_HEAD_DIM
    low = lax.broadcasted_iota(jnp.int32, (z.shape[0], lanes), 1) < RWKV_HEAD_DIM
    out = []
    for i in range(z.shape[1] // lanes):
        t = z[:, i * lanes:(i + 1) * lanes]
        s_lo = jnp.sum(jnp.where(low, t, 0.0), axis=-1, keepdims=True)
        s_hi = jnp.sum(jnp.where(low, 0.0, t), axis=-1, keepdims=True)
        out.append(jnp.where(low, s_lo, s_hi))
    return jnp.concatenate(out, axis=1)


def _sigmoid(x):
    return 1.0 / (1.0 + jnp.exp(-x))


def _softplus(x):
    return jnp.maximum(x, 0.0) + jnp.log(1.0 + jnp.exp(-jnp.abs(x)))


def _rms_norm(x, gain):
    return x * lax.rsqrt(jnp.mean(x * x, axis=-1, keepdims=True) + NORM_EPS) * gain


def _bd_rows(x, group_lanes):
    lane_head = lax.broadcasted_iota(jnp.int32, x.shape, 1) // group_lanes
    return jnp.concatenate(
        [jnp.where(lane_head == h, x, 0.0) for h in range(x.shape[1] // group_lanes)], axis=0)


def _mask_bd(z, group_lanes):
    lane_head = lax.broadcasted_iota(jnp.int32, (CHUNK, z.shape[1]), 1) // group_lanes
    return jnp.concatenate(
        [jnp.where(lane_head == h, z[h * CHUNK:(h + 1) * CHUNK], 0.0)
         for h in range(z.shape[1] // group_lanes)], axis=0)


def _chunk_rhs(z, n_heads):
    rows, width = z.shape
    n_chunks = rows // CHUNK
    zt = z.T
    rolled = [zt] + [pltpu.roll(zt, j * CHUNK, 1) for j in range(1, n_chunks)]
    lane_block = lax.broadcasted_iota(jnp.int32, (CHUNK, rows), 1) // CHUNK
    out = []
    for c in range(n_chunks):
        blocks = [jnp.where(lane_block == h,
                            rolled[(h - c) % n_chunks][h * CHUNK:(h + 1) * CHUNK], 0.0)[:, :width]
                  for h in range(n_heads)]
        out.append(jnp.concatenate(blocks, axis=0))
    return out


class _Fill:
    def __init__(self, pieces=(), calls=1):
        self.pieces = list(pieces)
        self.calls = calls

    def __call__(self):
        share = -(-len(self.pieces) // max(self.calls, 1))
        self.calls -= 1
        for _ in range(share):
            self.pieces.pop(0)()

    def drain(self):
        while self.pieces:
            self.pieces.pop(0)()


LOCAL_STAGES = 13
INV_BLOCK = 16


def _each(fn, *lists):
    return [fn(*args) for args in zip(*lists)]


def _prod(a_list, b_list, fill):
    out = _each(lambda a, b: _mm(a, _bd_rows(b, CHUNK)), a_list, b_list)
    fill()
    return out


def _add(a_list, b_list):
    return _each(lambda a, b: a + b, a_list, b_list)


def _tri_inverse(ns, fill):
    shape = ns[0].shape
    row = lax.broadcasted_iota(jnp.int32, shape, 0)
    col = lax.broadcasted_iota(jnp.int32, shape, 1) % CHUNK
    same16 = (row // INV_BLOCK) == (col // INV_BLOCK)
    same32 = (row // (2 * INV_BLOCK)) == (col // (2 * INV_BLOCK))
    n0 = [jnp.where(same16, n, 0.0) for n in ns]
    n1 = [jnp.where(same32, jnp.where(same16, 0.0, n), 0.0) for n in ns]
    n2 = [jnp.where(same32, 0.0, n) for n in ns]
    eye = jnp.where(row == col, 1.0, 0.0)
    t = [eye + n for n in n0]
    m = _prod(n0, n0, fill)
    for _ in range(2):
        tm = _prod(_each(lambda a, b: jnp.concatenate([a, b], axis=0), t, m), m, fill)
        t = _each(lambda a, b: a + b[:CHUNK], t, tm)
        m = [x[CHUNK:] for x in tm]
    t = _add(t, _prod(t, m, fill))
    stack = lambda a, b: jnp.concatenate([a, b], axis=0)
    nt = _prod(_each(stack, n1, n2), t, fill)
    x1 = [x[:CHUNK] for x in nt]
    tz = _prod(_each(lambda a, b: stack(a, b[CHUNK:]), t, nt), x1, fill)
    t = _each(lambda a, b: a + b[:CHUNK], t, tz)
    n2t = _each(lambda a, b: a[CHUNK:] + b[CHUNK:], nt, tz)
    return _add(t, _prod(t, n2t, fill))


def _rwkv_local(at, rt, wb, wk, v, fill):
    a = _each(lambda at_, rt_, wb_, wk_: _mm(
        jnp.concatenate([at_, rt_], axis=0), jnp.concatenate([wb_, wk_], axis=1)),
        at, rt, wb, wk)
    fill()
    row = lax.broadcasted_iota(jnp.int32, (CHUNK, 2 * QUAD), 0)
    col = lax.broadcasted_iota(jnp.int32, (CHUNK, 2 * QUAD), 1) % CHUNK
    a_a = [jnp.where(col < row, x[:CHUNK], 0.0) for x in a]
    a_r = [jnp.where(col <= row, x[CHUNK:], 0.0) for x in a]
    av = _each(lambda x, y, v_: _mm(jnp.concatenate([x[:, QUAD:], y[:, QUAD:]], axis=0),
                                    _bd_rows(v_, CHUNK)), a_a, a_r, v)
    fill()
    t = _tri_inverse([x[:, :QUAD] for x in a_a], fill)
    tu = _each(lambda t_, at_, av_: _mm(
        t_, jnp.concatenate([_bd_rows(at_, CHUNK),
                             _bd_rows(av_[:CHUNK].astype(BF16), CHUNK)], axis=1)),
        t, at, av)
    fill()
    return ([x[:, :QUAD] for x in tu], [x[:, QUAD:] for x in tu],
            [x[:, :QUAD] for x in a_r], [x[CHUNK:] for x in av])


def _ret_local(q, k, k_w, v, intra_decay, fill):
    scores = _each(lambda q_, k_, d_: _mm(q_, k_) * d_, q, k, intra_decay)
    fill()
    intra = _each(lambda s_, v_: _mm(s_, _bd_rows(v_, RET_V_DIM)), scores, v)
    fill()
    kv = _each(lambda k_, v_: _mask_bd(_mm(k_, v_, TN), RET_V_DIM), k_w, v)
    fill()
    return intra, kv


def _rope(x, cos, sin_signed):
    half = RET_QK_DIM // 2
    width = x.shape[1]
    lane = lax.broadcasted_iota(jnp.int32, x.shape, 1) % RET_QK_DIM
    partner = jnp.where(lane < half, pltpu.roll(x, width - half, 1), pltpu.roll(x, half, 1))
    return x * cos + partner * sin_signed


def _mixer_kernel(x_ref, gain_ref, win_ref, mu_ref, lora_ref, glora_ref, vecs_ref, wout_ref,
                  tri_ref, cos_ref, sin_ref, dd_ref, qw_ref, kw_ref, g64_ref,
                  o_ref, carry_ref, srw_ref, sret_ref, *, seqs, rows):
    @pl.when(pl.program_id(1) == 0)
    def _():
        carry_ref[...] = jnp.zeros_like(carry_ref)
        srw_ref[...] = jnp.zeros_like(srw_ref)
        sret_ref[...] = jnp.zeros_like(sret_ref)

    n_chunks = rows // CHUNK
    w = RWKV_WIDTH
    vecs = vecs_ref[...]
    w0, a0, k_k, k_a, r_k, ln_w, ln_b = (vecs[i:i + 1] for i in range(7))
    intra_decay = [dd_ref[:, p * RET_QK_PAIR:(p + 1) * RET_QK_PAIR] for p in range(RET_PAIRS)]
    chunk_decay = [g64_ref[:, p * RET_V_PAIR:(p + 1) * RET_V_PAIR] for p in range(RET_PAIRS)]

    def pieces(z):
        return [z[c * CHUNK:(c + 1) * CHUNK, qd * QUAD:(qd + 1) * QUAD]
                for c in range(n_chunks) for qd in range(N_QUADS)]

    def pairs(z, lanes):
        return [z[c * CHUNK:(c + 1) * CHUNK, p * lanes:(p + 1) * lanes]
                for c in range(n_chunks) for p in range(RET_PAIRS)]

    def front_stages(sq, p_blocks):
        t, f = {}, {}
        q0 = RWKV_IN

        def shift():
            p = t["p"] = jnp.concatenate(p_blocks, axis=1)
            pr = p[:, :RWKV_IN]
            first = lax.broadcasted_iota(jnp.int32, pr.shape, 0) == 0
            prev = jnp.where(first, carry_ref[sq], pltpu.roll(pr, 1, 0))
            carry_ref[sq] = pr[rows - 1:rows, :]
            t["ps"] = pr + mu_ref[...] * (prev - pr)

        def loras():
            ps = t["ps"]
            wa = ps[:, 3 * w:3 * w + DECAY_LORA + ICL_LORA]
            gd = ps[:, 3 * w + DECAY_LORA + ICL_LORA:]
            is_decay = lax.broadcasted_iota(jnp.int32, wa.shape, 1) < DECAY_LORA
            lora = _dot(jnp.where(is_decay, jnp.tanh(wa), wa).astype(BF16), lora_ref[...])
            log_w = -_softplus(-(w0 + lora[:, :w])) - 0.5
            t["log_decay"] = -jnp.exp(log_w)
            t["a"] = _sigmoid(a0 + lora[:, w:])
            f["g"] = _dot(_sigmoid(gd).astype(BF16), glora_ref[...])

        def keys():
            ps = t["ps"]
            r, k, v = ps[:, :w], ps[:, w:2 * w], ps[:, 2 * w:3 * w]
            kk = k * k_k
            t["kk"] = kk * lax.rsqrt(jnp.maximum(_head_sum(kk * kk), 1e-24))
            t["k_mod"] = k * (1.0 + (t["a"] - 1.0) * k_a)
            f["bonus"] = _head_sum(r * t["k_mod"] * r_k) * v
            f["v"] = pieces(v.astype(BF16))

        def decays():
            cum = _const_mm(tri_ref[...], t["log_decay"], 2)
            tot = jnp.concatenate(
                [jnp.broadcast_to(cum[(i + 1) * CHUNK - 1:(i + 1) * CHUNK], (CHUNK, w))
                 for i in range(n_chunks)], axis=0)
            t["cum"], t["tot"] = cum, tot
            tot_t = [jnp.exp(tot[:, qd * QUAD:(qd + 1) * QUAD]).T for qd in range(N_QUADS)]
            f["w_tot"] = [tot_t[qd][:, c * CHUNK:c * CHUNK + 1]
                          for c in range(n_chunks) for qd in range(N_QUADS)]
            f["at"] = pieces((-t["kk"] * jnp.exp(cum - t["log_decay"])).astype(BF16))
            f["rt"] = pieces((t["ps"][:, :w] * jnp.exp(cum)).astype(BF16))

        def scaled():
            kka = t["kk"] * t["a"]
            w_inv = jnp.exp(-t["cum"])
            w_end = jnp.exp(t["tot"] - t["cum"])
            scale = lambda z, s: pieces((z * s).astype(BF16))
            f["bw"], f["kw"] = scale(kka, w_end), scale(t["k_mod"], w_end)

            def rhs(z):
                per_quad = [_chunk_rhs(z[:, qd * QUAD:(qd + 1) * QUAD], GROUP)
                            for qd in range(N_QUADS)]
                return [per_quad[qd][c] for c in range(n_chunks) for qd in range(N_QUADS)]
            f["wb"], f["wk"] = rhs(kka * w_inv), rhs(t["k_mod"] * w_inv)

        def retention():
            p = t["p"]
            cos = cos_ref[...]
            sin = sin_ref[...]
            rq = _rope(p[:, q0:q0 + RET_QK_WIDTH], cos, sin)
            rk = _rope(p[:, q0 + RET_QK_WIDTH:q0 + 2 * RET_QK_WIDTH], cos, sin) \
                * (RET_QK_DIM ** -0.5)
            qk = lambda z: pairs(z.astype(BF16), RET_QK_PAIR)
            f["rq"] = qk(rq)
            per_pair = [_chunk_rhs(rk[:, p_ * RET_QK_PAIR:(p_ + 1) * RET_QK_PAIR], 2)
                        for p_ in range(RET_PAIRS)]
            f["rk"] = [per_pair[p_][c] for c in range(n_chunks) for p_ in range(RET_PAIRS)]
            f["rqw"], f["rkw"] = qk(rq * qw_ref[...]), qk(rk * kw_ref[...])
            f["rv"] = pairs(p[:, q0 + 2 * RET_QK_WIDTH:q0 + 2 * RET_QK_WIDTH + RET_V_WIDTH]
                            .astype(BF16), RET_V_PAIR)
            f["rg"] = p[:, q0 + 2 * RET_QK_WIDTH + RET_V_WIDTH:]

        return [shift, loras, keys, decays, scaled, retention], f

    def local(f, fill):
        a_hat, u_loc, a_rb, a_rk_v = _rwkv_local(f["at"], f["rt"], f["wb"], f["wk"], f["v"], fill)
        intra, kv = _ret_local(f["rq"], f["rk"], f["rkw"], f["rv"], intra_decay * n_chunks, fill)
        return dict(a_hat=a_hat, u_loc=u_loc, a_rb=a_rb, a_rk_v=a_rk_v, intra=intra, kv=kv)

    def state_stages(sq, f, loc):
        st = dict(rw=[srw_ref[sq * N_QUADS + qd] for qd in range(N_QUADS)],
                  ret=[sret_ref[sq * RET_PAIRS + p] for p in range(RET_PAIRS)])
        res = dict(y=[None] * n_chunks, o=[None] * n_chunks)
        tmp = {}

        def quads(name, c):
            return f[name][c * N_QUADS:(c + 1) * N_QUADS] if name in f else \
                loc[name][c * N_QUADS:(c + 1) * N_QUADS]

        def read_state(c):
            tmp["ps"] = _each(lambda a_, r_, s_: _mm(
                jnp.concatenate([a_.astype(BF16), r_], axis=0), s_),
                              quads("a_hat", c), quads("rt", c), st["rw"])
            ps_ = slice(c * RET_PAIRS, (c + 1) * RET_PAIRS)
            res["o"][c] = jnp.concatenate(
                _each(lambda i_, q_, s_: i_ + _mm(q_, s_), loc["intra"][ps_], f["rqw"][ps_],
                      st["ret"]), axis=1)
            st["ret"] = _each(lambda s_, d_, kv_: s_ * d_ + kv_, st["ret"], chunk_decay,
                              loc["kv"][ps_])

        def update_state(c):
            tmp["u"] = _each(lambda p_, u_: p_[:CHUNK] + u_, tmp["ps"], quads("u_loc", c))
            upd = _each(lambda u_, v_, bw_, kw_: _mm(jnp.concatenate([bw_, kw_], axis=0),
                                                     jnp.concatenate([u_.astype(BF16), v_], axis=0),
                                                     TN),
                        tmp["u"], quads("v", c), quads("bw", c), quads("kw", c))
            st["rw"] = _each(lambda s_, w_, d_: s_ * w_ + _mask_bd(d_, CHUNK),
                             st["rw"], quads("w_tot", c), upd)
            if c == n_chunks - 1:
                for qd in range(N_QUADS):
                    srw_ref[sq * N_QUADS + qd] = st["rw"][qd]
                for p in range(RET_PAIRS):
                    sret_ref[sq * RET_PAIRS + p] = st["ret"][p]

        def emit_y(c, ps, u):
            y = _each(lambda p_, a_, u_, b_: p_[CHUNK:] + b_ + _mm(a_, _bd_rows(u_, CHUNK)),
                      ps, quads("a_rb", c), u, quads("a_rk_v", c))
            res["y"][c] = jnp.concatenate(y, axis=1)

        stages = []
        for c in range(n_chunks):
            stages.append(functools.partial(read_state, c))
            stages.append(functools.partial(update_state, c))
            stages.append(lambda c=c: emit_y(c, tmp["ps"], tmp["u"]))
        return stages, res

    def output_stages(sq, x, f, res):
        tmp = {}
        blocks = []

        def gates():
            y = jnp.concatenate(res["y"], axis=0)
            inv_n = 1.0 / RWKV_HEAD_DIM
            yc = y - _head_sum(y) * inv_n
            var = _head_sum(yc * yc) * inv_n
            y = (yc * lax.rsqrt(var + LN_X_EPS) * ln_w + ln_b + f["bonus"]) * f["g"]
            o = jnp.concatenate(res["o"], axis=0)
            o = jnp.concatenate(
                [o[:, i * RET_V_DIM:(i + 1) * RET_V_DIM]
                 * lax.rsqrt(jnp.mean(jnp.square(o[:, i * RET_V_DIM:(i + 1) * RET_V_DIM]),
                                      axis=-1, keepdims=True) + NORM_EPS)
                 for i in range(RET_HEADS)], axis=1)
            rg = f["rg"]
            o = o * (rg * _sigmoid(rg))
            tmp["mixed"] = jnp.concatenate([y, o], axis=1).astype(BF16)

        def block(j):
            cs = slice(j * MXU_TILE, (j + 1) * MXU_TILE)
            blocks.append(x[:, cs] + _dot(tmp["mixed"], wout_ref[:, cs]))
            if j == D_MODEL // MXU_TILE - 1:
                o_ref[sq] = jnp.concatenate(blocks, axis=1)

        return [gates] + [functools.partial(block, j) for j in range(D_MODEL // MXU_TILE)]

    def proj_stages(h, out):
        def block(j):
            out.append(_dot(h, win_ref[:, j * MXU_TILE:(j + 1) * MXU_TILE]))
        return [functools.partial(block, j) for j in range(PROJ_WIDTH // MXU_TILE)]

    xs = [x_ref[sq] for sq in range(seqs)]
    hs = [_rms_norm(x, gain_ref[...]).astype(BF16) for x in xs]
    def interleave(a, b):
        out = []
        for i in range(max(len(a), len(b))):
            out += a[i:i + 1] + b[i:i + 1]
        return out

    p_blocks = [[] for _ in range(seqs)]
    fronts = [front_stages(sq, p_blocks[sq]) for sq in range(seqs)]
    _Fill(proj_stages(hs[0], p_blocks[0])).drain()
    _Fill(interleave(fronts[0][0], proj_stages(hs[1], p_blocks[1]) if seqs > 1 else [])).drain()
    state_todo, output_todo = [], []
    for sq in range(seqs):
        f = fronts[sq][1]
        ahead = fronts[sq + 1][0] if sq + 1 < seqs else []
        ahead = interleave(ahead, proj_stages(hs[sq + 2], p_blocks[sq + 2]) if sq + 2 < seqs else [])
        fill = _Fill(interleave(interleave(state_todo, output_todo), ahead), LOCAL_STAGES)
        loc = local(f, fill)
        fill.drain()
        stages, res = state_stages(sq, f, loc)
        output_todo = output_stages(sq - 1, xs[sq - 1], *done) if sq else []
        state_todo, done = stages, (f, res)
    _Fill(interleave(state_todo, output_todo)).drain()
    _Fill(output_stages(seqs - 1, xs[seqs - 1], *done)).drain()


def _ffn_kernel(x_ref, gain_ref, wg_ref, wu_ref, wd_ref, fin_ref, o_ref, *, final_norm):
    x = x_ref[...]
    h = _rms_norm(x, gain_ref[...]).astype(BF16)
    gate = _dot(h, wg_ref[...])
    up = _dot(h, wu_ref[...])
    act = (gate * _sigmoid(gate) * up).astype(BF16)
    y = x + 0.5 * _dot(act, wd_ref[...])
    if final_norm:
        y = _rms_norm(y, fin_ref[...])
    o_ref[...] = y


def _resident(shape):
    return pl.BlockSpec(shape, lambda *_: (0,) * len(shape), pipeline_mode=pl.Buffered(1))


def _layer(shape, l):
    return pl.BlockSpec((None,) + shape, lambda *_: (l,) + (0,) * len(shape),
                        pipeline_mode=pl.Buffered(1))


def _ffn(x2, l, gain, wg, wu, wd, fin, final_norm):
    t = x2.shape[0]
    rows = min(FFN_ROWS, t)
    assert t % rows == 0
    return pl.pallas_call(
        functools.partial(_ffn_kernel, final_norm=final_norm),
        out_shape=jax.ShapeDtypeStruct(x2.shape, F32),
        grid=(t // rows,),
        in_specs=[pl.BlockSpec((rows, D_MODEL), lambda i: (i, 0)),
                  _layer((1, D_MODEL), l), _layer((D_MODEL, D_FF), l), _layer((D_MODEL, D_FF), l),
                  _layer((D_FF, D_MODEL), l), _resident((1, D_MODEL))],
        out_specs=pl.BlockSpec((rows, D_MODEL), lambda i: (i, 0)),
        compiler_params=pltpu.CompilerParams(
            dimension_semantics=("arbitrary",), vmem_limit_bytes=VMEM_LIMIT_BYTES),
        name="ffn_final" if final_norm else "ffn",
    )(x2, gain, wg, wu, wd, fin)


def _mixer_tables(seq, rows):
    idx = jnp.arange(rows)
    same_chunk = (idx[:, None] // CHUNK) == (idx[None, :] // CHUNK)
    tri = (same_chunk & (idx[None, :] <= idx[:, None])).astype(BF16)
    half = RET_QK_DIM // 2
    inv_freq = 1.0 / (ROPE_BASE ** jnp.linspace(0.0, 1.0, half, dtype=F32))
    ang = jnp.arange(seq, dtype=F32)[:, None] * inv_freq[None, :]
    cos = jnp.tile(jnp.cos(ang), (1, 2 * RET_HEADS))
    sin = jnp.tile(jnp.concatenate([-jnp.sin(ang), jnp.sin(ang)], axis=1), (1, RET_HEADS))
    log_gamma = jnp.log(1.0 - jnp.power(2.0, -5.0 - jnp.arange(RET_HEADS, dtype=F32)))
    pos = jnp.arange(CHUNK, dtype=F32)
    dist = jnp.abs(pos[:, None] - pos[None, :])
    dd = jnp.exp(log_gamma[:, None, None] * dist)
    dd = jnp.transpose(dd, (1, 0, 2)).reshape(CHUNK, RET_HEADS * CHUNK)
    query_w = jnp.exp(log_gamma[:, None] * (pos + 1.0)[None, :])
    key_w = jnp.exp(log_gamma[:, None] * (CHUNK - 1.0 - pos)[None, :])
    qw = jnp.tile(jnp.repeat(query_w.T, RET_QK_DIM, axis=1), (rows // CHUNK, 1))
    kw = jnp.tile(jnp.repeat(key_w.T, RET_QK_DIM, axis=1), (rows // CHUNK, 1))
    g64 = jnp.repeat(jnp.exp(log_gamma * CHUNK), RET_V_DIM)[None, :]
    return tri, cos, sin, dd, qw, kw, g64


def _mixer(x, l, gain, w_in, mu, lora, glora, vecs, w_out, tables):
    b, seq, _ = x.shape
    rows = min(MIX_ROWS, seq)
    seqs = MIX_SEQS if b % MIX_SEQS == 0 else 1
    assert seq % rows == 0 and rows % CHUNK == 0
    tri, cos, sin, dd, qw, kw, g64 = tables
    return pl.pallas_call(
        functools.partial(_mixer_kernel, seqs=seqs, rows=rows),
        out_shape=jax.ShapeDtypeStruct(x.shape, F32),
        grid=(b // seqs, seq // rows),
        in_specs=[pl.BlockSpec((seqs, rows, D_MODEL), lambda i, j: (i, j, 0)),
                  _layer((1, D_MODEL), l), _layer((D_MODEL, PROJ_WIDTH), l),
                  _layer((1, RWKV_IN), l), _layer((DECAY_LORA + ICL_LORA, 2 * RWKV_WIDTH), l),
                  _layer((GATE_LORA, RWKV_WIDTH), l), _layer((8, RWKV_WIDTH), l),
                  _layer((D_MODEL, D_MODEL), l),
                  _resident((rows, rows)),
                  pl.BlockSpec((rows, RET_QK_WIDTH), lambda i, j: (j, 0)),
                  pl.BlockSpec((rows, RET_QK_WIDTH), lambda i, j: (j, 0)),
                  _resident((CHUNK, RET_QK_WIDTH)), _resident((rows, RET_QK_WIDTH)),
                  _resident((rows, RET_QK_WIDTH)), _resident((1, RET_V_WIDTH))],
        out_specs=pl.BlockSpec((seqs, rows, D_MODEL), lambda i, j: (i, j, 0)),
        scratch_shapes=[pltpu.VMEM((seqs, 1, RWKV_IN), F32),
                        pltpu.VMEM((seqs * N_QUADS, QUAD, QUAD), F32),
                        pltpu.VMEM((seqs * RET_PAIRS, RET_QK_PAIR, RET_V_PAIR), F32)],
        compiler_params=pltpu.CompilerParams(
            dimension_semantics=("arbitrary", "arbitrary"), vmem_limit_bytes=VMEM_LIMIT_BYTES),
        name="mixer",
    )(x, gain, w_in, mu, lora, glora, vecs, w_out, tri, cos, sin, dd, qw, kw, g64)


def kernel(x, ffn1_norm, ffn1_w_gate, ffn1_w_up, ffn1_w_down, mix_norm, w_in, shift_mu, w0, w_lora_up, a0, a_lora_up, g_lora_up, k_k, k_a, r_k, ln_x_w, ln_x_b, w_out, ffn2_norm, ffn2_w_gate, ffn2_w_up, ffn2_w_down, final_norm):
    b, seq, d = x.shape
    depth = w_in.shape[0]
    tables = _mixer_tables(seq, min(MIX_ROWS, seq))
    bf = lambda z: z.astype(BF16)
    row = lambda z: z.reshape(depth, 1, -1)
    ffn1 = (row(ffn1_norm), bf(ffn1_w_gate), bf(ffn1_w_up), bf(ffn1_w_down))
    ffn2 = (row(ffn2_norm), bf(ffn2_w_gate), bf(ffn2_w_up), bf(ffn2_w_down))
    fin = final_norm[None, :]
    zeros = jnp.zeros((depth, DECAY_LORA, RWKV_WIDTH), F32)
    lora = bf(jnp.concatenate(
        [jnp.concatenate([w_lora_up, zeros], axis=2),
         jnp.concatenate([zeros, a_lora_up], axis=2)], axis=1))
    vecs = jnp.stack([w0, a0, k_k, k_a, r_k.reshape(depth, -1), ln_x_w, ln_x_b,
                      jnp.zeros((depth, RWKV_WIDTH), F32)], axis=1)
    mix = (row(mix_norm), bf(w_in), row(shift_mu), lora, bf(g_lora_up), vecs, bf(w_out))
    for l in range(depth):
        x = _ffn(x.reshape(b * seq, d), l, *ffn1, fin, False)
        x = _mixer(x.reshape(b, seq, d), l, *mix, tables)
        x = _ffn(x.reshape(b * seq, d), l, *ffn2, fin, l == depth - 1)
    return x.reshape(b, seq, d)
```

```python
import functools

import jax
import jax.numpy as jnp
from jax import lax
from jax.experimental import pallas as pl
from jax.experimental.pallas import tpu as pltpu

D_MODEL = 1024
D_FF = 2816
CHUNK = 64
RWKV_HEADS = 8
RWKV_HEAD_DIM = 64
RWKV_WIDTH = RWKV_HEADS * RWKV_HEAD_DIM
DECAY_LORA = 64
ICL_LORA = 64
GATE_LORA = 128
RWKV_IN = 3 * RWKV_WIDTH + DECAY_LORA + ICL_LORA + GATE_LORA
RET_HEADS = 4
RET_QK_DIM = 64
RET_V_DIM = 128
RET_QK_WIDTH = RET_HEADS * RET_QK_DIM
RET_V_WIDTH = RET_HEADS * RET_V_DIM
RET_IN = 2 * RET_QK_WIDTH + 2 * RET_V_WIDTH
PROJ_WIDTH = RWKV_IN + RET_IN
ROPE_BASE = 10000.0
NORM_EPS = 1e-6
LN_X_EPS = 64e-5

GROUP = 4
QUAD = GROUP * RWKV_HEAD_DIM
N_QUADS = RWKV_HEADS // GROUP
MXU_TILE = 256
RET_PAIRS = RET_HEADS // 2
RET_QK_PAIR = 2 * RET_QK_DIM
RET_V_PAIR = 2 * RET_V_DIM
FFN_ROWS = 1024
MIX_ROWS = 256
MIX_SEQS = 4
VMEM_LIMIT_BYTES = 58 * 1024 * 1024
CAST_ROWS = 1024

F32 = jnp.float32
BF16 = jnp.bfloat16
NN = (((1,), (0,)), ((), ()))
TN = (((0,), (0,)), ((), ()))


def _dot(a, b, dims=NN):
    return lax.dot_general(a, b, dims, preferred_element_type=F32)


def _mm(a, b, dims=NN):
    return _dot(a.astype(BF16), b.astype(BF16), dims)


def _const_mm(c_bf16, x, terms):
    acc = None
    rem = x
    for _ in range(terms):
        piece = rem.astype(BF16)
        rem = rem - piece.astype(F32)
        d = _dot(c_bf16, piece)
        acc = d if acc is None else acc + d
    return acc


def _head_sum(z):
    lanes = 2 * RWKV_HEAD_DIM
    low = lax.broadcasted_iota(jnp.int32, (z.shape[0], lanes), 1) < RWKV_HEAD_DIM
    out = []
    for i in range(z.shape[1] // lanes):
        t = z[:, i * lanes:(i + 1) * lanes]
        s_lo = jnp.sum(jnp.where(low, t, 0.0), axis=-1, keepdims=True)
        s_hi = jnp.sum(jnp.where(low, 0.0, t), axis=-1, keepdims=True)
        out.append(jnp.where(low, s_lo, s_hi))
    return jnp.concatenate(out, axis=1)


def _sigmoid(x):
    return 1.0 / (1.0 + jnp.exp(-x))


def _softplus(x):
    return jnp.maximum(x, 0.0) + jnp.log(1.0 + jnp.exp(-jnp.abs(x)))


def _rms_norm(x, gain):
    return x * lax.rsqrt(jnp.mean(x * x, axis=-1, keepdims=True) + NORM_EPS) * gain


def _bd_rows(x, group_lanes):
    lane_head = lax.broadcasted_iota(jnp.int32, x.shape, 1) // group_lanes
    return jnp.concatenate(
        [jnp.where(lane_head == h, x, 0.0) for h in range(x.shape[1] // group_lanes)], axis=0)


def _mask_bd(z, group_lanes):
    lane_head = lax.broadcasted_iota(jnp.int32, (CHUNK, z.shape[1]), 1) // group_lanes
    return jnp.concatenate(
        [jnp.where(lane_head == h, z[h * CHUNK:(h + 1) * CHUNK], 0.0)
         for h in range(z.shape[1] // group_lanes)], axis=0)


def _chunk_rhs(z, n_heads):
    rows, width = z.shape
    n_chunks = rows // CHUNK
    zt = z.T
    rolled = [zt] + [pltpu.roll(zt, j * CHUNK, 1) for j in range(1, n_chunks)]
    lane_block = lax.broadcasted_iota(jnp.int32, (CHUNK, rows), 1) // CHUNK
    out = []
    for c in range(n_chunks):
        blocks = [jnp.where(lane_block == h,
                            rolled[(h - c) % n_chunks][h * CHUNK:(h + 1) * CHUNK], 0.0)[:, :width]
                  for h in range(n_heads)]
        out.append(jnp.concatenate(blocks, axis=0))
    return out


class _Fill:
    def __init__(self, pieces=(), calls=1):
        self.pieces = list(pieces)
        self.calls = calls

    def __call__(self):
        share = -(-len(self.pieces) // max(self.calls, 1))
        self.calls -= 1
        for _ in range(share):
            self.pieces.pop(0)()

    def drain(self):
        while self.pieces:
            self.pieces.pop(0)()


LOCAL_STAGES = 13
INV_BLOCK = 16


def _each(fn, *lists):
    return [fn(*args) for args in zip(*lists)]


def _prod(a_list, b_list, fill):
    out = _each(lambda a, b: _mm(a, _bd_rows(b, CHUNK)), a_list, b_list)
    fill()
    return out


def _add(a_list, b_list):
    return _each(lambda a, b: a + b, a_list, b_list)


def _tri_inverse(ns, fill):
    shape = ns[0].shape
    row = lax.broadcasted_iota(jnp.int32, shape, 0)
    col = lax.broadcasted_iota(jnp.int32, shape, 1) % CHUNK
    same16 = (row // INV_BLOCK) == (col // INV_BLOCK)
    same32 = (row // (2 * INV_BLOCK)) == (col // (2 * INV_BLOCK))
    n0 = [jnp.where(same16, n, 0.0) for n in ns]
    n1 = [jnp.where(same32, jnp.where(same16, 0.0, n), 0.0) for n in ns]
    n2 = [jnp.where(same32, 0.0, n) for n in ns]
    eye = jnp.where(row == col, 1.0, 0.0)
    t = [eye + n for n in n0]
    m = _prod(n0, n0, fill)
    for _ in range(2):
        tm = _prod(_each(lambda a, b: jnp.concatenate([a, b], axis=0), t, m), m, fill)
        t = _each(lambda a, b: a + b[:CHUNK], t, tm)
        m = [x[CHUNK:] for x in tm]
    t = _add(t, _prod(t, m, fill))
    stack = lambda a, b: jnp.concatenate([a, b], axis=0)
    nt = _prod(_each(stack, n1, n2), t, fill)
    x1 = [x[:CHUNK] for x in nt]
    tz = _prod(_each(lambda a, b: stack(a, b[CHUNK:]), t, nt), x1, fill)
    t = _each(lambda a, b: a + b[:CHUNK], t, tz)
    n2t = _each(lambda a, b: a[CHUNK:] + b[CHUNK:], nt, tz)
    return _add(t, _prod(t, n2t, fill))


def _rwkv_local(at, rt, wb, wk, v, fill):
    a = _each(lambda at_, rt_, wb_, wk_: _mm(
        jnp.concatenate([at_, rt_], axis=0), jnp.concatenate([wb_, wk_], axis=1)),
        at, rt, wb, wk)
    fill()
    row = lax.broadcasted_iota(jnp.int32, (CHUNK, 2 * QUAD), 0)
    col = lax.broadcasted_iota(jnp.int32, (CHUNK, 2 * QUAD), 1) % CHUNK
    a_a = [jnp.where(col < row, x[:CHUNK], 0.0) for x in a]
    a_r = [jnp.where(col <= row, x[CHUNK:], 0.0) for x in a]
    av = _each(lambda x, y, v_: _mm(jnp.concatenate([x[:, QUAD:], y[:, QUAD:]], axis=0),
                                    _bd_rows(v_, CHUNK)), a_a, a_r, v)
    fill()
    t = _tri_inverse([x[:, :QUAD] for x in a_a], fill)
    tu = _each(lambda t_, at_, av_: _mm(
        t_, jnp.concatenate([_bd_rows(at_, CHUNK),
                             _bd_rows(av_[:CHUNK].astype(BF16), CHUNK)], axis=1)),
        t, at, av)
    fill()
    return ([x[:, :QUAD] for x in tu], [x[:, QUAD:] for x in tu],
            [x[:, :QUAD] for x in a_r], [x[CHUNK:] for x in av])


def _ret_local(q, k, k_w, v, intra_decay, fill):
    scores = _each(lambda q_, k_, d_: _mm(q_, k_) * d_, q, k, intra_decay)
    fill()
    intra = _each(lambda s_, v_: _mm(s_, _bd_rows(v_, RET_V_DIM)), scores, v)
    fill()
    kv = _each(lambda k_, v_: _mask_bd(_mm(k_, v_, TN), RET_V_DIM), k_w, v)
    fill()
    return intra, kv


def _rope(x, cos, sin_signed):
    half = RET_QK_DIM // 2
    width = x.shape[1]
    lane = lax.broadcasted_iota(jnp.int32, x.shape, 1) % RET_QK_DIM
    partner = jnp.where(lane < half, pltpu.roll(x, width - half, 1), pltpu.roll(x, half, 1))
    return x * cos + partner * sin_signed


def _mixer_kernel(x_ref, gain_ref, win_ref, mu_ref, lora_ref, glora_ref, vecs_ref, wout_ref,
                  tri_ref, cos_ref, sin_ref, dd_ref, qw_ref, kw_ref, g64_ref,
                  o_ref, carry_ref, srw_ref, sret_ref, *, seqs, rows):
    @pl.when(pl.program_id(1) == 0)
    def _():
        carry_ref[...] = jnp.zeros_like(carry_ref)
        srw_ref[...] = jnp.zeros_like(srw_ref)
        sret_ref[...] = jnp.zeros_like(sret_ref)

    n_chunks = rows // CHUNK
    w = RWKV_WIDTH
    vecs = vecs_ref[...]
    w0, a0, k_k, k_a, r_k, ln_w, ln_b = (vecs[i:i + 1] for i in range(7))
    intra_decay = [dd_ref[:, p * RET_QK_PAIR:(p + 1) * RET_QK_PAIR] for p in range(RET_PAIRS)]
    chunk_decay = [g64_ref[:, p * RET_V_PAIR:(p + 1) * RET_V_PAIR] for p in range(RET_PAIRS)]

    def pieces(z):
        return [z[c * CHUNK:(c + 1) * CHUNK, qd * QUAD:(qd + 1) * QUAD]
                for c in range(n_chunks) for qd in range(N_QUADS)]

    def pairs(z, lanes):
        return [z[c * CHUNK:(c + 1) * CHUNK, p * lanes:(p + 1) * lanes]
                for c in range(n_chunks) for p in range(RET_PAIRS)]

    def front_stages(sq, p_blocks):
        t, f = {}, {}
        q0 = RWKV_IN

        def shift():
            p = t["p"] = jnp.concatenate(p_blocks, axis=1)
            pr = p[:, :RWKV_IN]
            first = lax.broadcasted_iota(jnp.int32, pr.shape, 0) == 0
            prev = jnp.where(first, carry_ref[sq], pltpu.roll(pr, 1, 0))
            carry_ref[sq] = pr[rows - 1:rows, :]
            t["ps"] = pr + mu_ref[...] * (prev - pr)

        def loras():
            ps = t["ps"]
            wa = ps[:, 3 * w:3 * w + DECAY_LORA + ICL_LORA]
            gd = ps[:, 3 * w + DECAY_LORA + ICL_LORA:]
            is_decay = lax.broadcasted_iota(jnp.int32, wa.shape, 1) < DECAY_LORA
            lora = _dot(jnp.where(is_decay, jnp.tanh(wa), wa).astype(BF16), lora_ref[...])
            log_w = -_softplus(-(w0 + lora[:, :w])) - 0.5
            t["log_decay"] = -jnp.exp(log_w)
            t["a"] = _sigmoid(a0 + lora[:, w:])
            f["g"] = _dot(_sigmoid(gd).astype(BF16), glora_ref[...])

        def keys():
            ps = t["ps"]
            r, k, v = ps[:, :w], ps[:, w:2 * w], ps[:, 2 * w:3 * w]
            kk = k * k_k
            t["kk"] = kk * lax.rsqrt(jnp.maximum(_head_sum(kk * kk), 1e-24))
            t["k_mod"] = k * (1.0 + (t["a"] - 1.0) * k_a)
            f["bonus"] = _head_sum(r * t["k_mod"] * r_k) * v
            f["v"] = pieces(v.astype(BF16))

        def decays():
            cum = _const_mm(tri_ref[...], t["log_decay"], 2)
            tot = jnp.concatenate(
                [jnp.broadcast_to(cum[(i + 1) * CHUNK - 1:(i + 1) * CHUNK], (CHUNK, w))
                 for i in range(n_chunks)], axis=0)
            t["cum"], t["tot"] = cum, tot
            tot_t = [jnp.exp(tot[:, qd * QUAD:(qd + 1) * QUAD]).T for qd in range(N_QUADS)]
            f["w_tot"] = [tot_t[qd][:, c * CHUNK:c * CHUNK + 1]
                          for c in range(n_chunks) for qd in range(N_QUADS)]
            f["at"] = pieces((-t["kk"] * jnp.exp(cum - t["log_decay"])).astype(BF16))
            f["rt"] = pieces((t["ps"][:, :w] * jnp.exp(cum)).astype(BF16))

        def scaled():
            kka = t["kk"] * t["a"]
            w_inv = jnp.exp(-t["cum"])
            w_end = jnp.exp(t["tot"] - t["cum"])
            scale = lambda z, s: pieces((z * s).astype(BF16))
            f["bw"], f["kw"] = scale(kka, w_end), scale(t["k_mod"], w_end)

            def rhs(z):
                per_quad = [_chunk_rhs(z[:, qd * QUAD:(qd + 1) * QUAD], GROUP)
                            for qd in range(N_QUADS)]
                return [per_quad[qd][c] for c in range(n_chunks) for qd in range(N_QUADS)]
            f["wb"], f["wk"] = rhs(kka * w_inv), rhs(t["k_mod"] * w_inv)

        def retention():
            p = t["p"]
            cos = cos_ref[...]
            sin = sin_ref[...]
            rq = _rope(p[:, q0:q0 + RET_QK_WIDTH], cos, sin)
            rk = _rope(p[:, q0 + RET_QK_WIDTH:q0 + 2 * RET_QK_WIDTH], cos, sin) \
                * (RET_QK_DIM ** -0.5)
            qk = lambda z: pairs(z.astype(BF16), RET_QK_PAIR)
            f["rq"] = qk(rq)
            per_pair = [_chunk_rhs(rk[:, p_ * RET_QK_PAIR:(p_ + 1) * RET_QK_PAIR], 2)
                        for p_ in range(RET_PAIRS)]
            f["rk"] = [per_pair[p_][c] for c in range(n_chunks) for p_ in range(RET_PAIRS)]
            f["rqw"], f["rkw"] = qk(rq * qw_ref[...]), qk(rk * kw_ref[...])
            f["rv"] = pairs(p[:, q0 + 2 * RET_QK_WIDTH:q0 + 2 * RET_QK_WIDTH + RET_V_WIDTH]
                            .astype(BF16), RET_V_PAIR)
            f["rg"] = p[:, q0 + 2 * RET_QK_WIDTH + RET_V_WIDTH:]

        return [shift, loras, keys, decays, scaled, retention], f

    def local(f, fill):
        a_hat, u_loc, a_rb, a_rk_v = _rwkv_local(f["at"], f["rt"], f["wb"], f["wk"], f["v"], fill)
        intra, kv = _ret_local(f["rq"], f["rk"], f["rkw"], f["rv"], intra_decay * n_chunks, fill)
        return dict(a_hat=a_hat, u_loc=u_loc, a_rb=a_rb, a_rk_v=a_rk_v, intra=intra, kv=kv)

    def state_stages(sq, f, loc):
        st = dict(rw=[srw_ref[sq * N_QUADS + qd] for qd in range(N_QUADS)],
                  ret=[sret_ref[sq * RET_PAIRS + p] for p in range(RET_PAIRS)])
        res = dict(y=[None] * n_chunks, o=[None] * n_chunks)
        tmp = {}

        def quads(name, c):
            return f[name][c * N_QUADS:(c + 1) * N_QUADS] if name in f else \
                loc[name][c * N_QUADS:(c + 1) * N_QUADS]

        def read_state(c):
            tmp["ps"] = _each(lambda a_, r_, s_: _mm(
                jnp.concatenate([a_.astype(BF16), r_], axis=0), s_),
                              quads("a_hat", c), quads("rt", c), st["rw"])
            ps_ = slice(c * RET_PAIRS, (c + 1) * RET_PAIRS)
            res["o"][c] = jnp.concatenate(
                _each(lambda i_, q_, s_: i_ + _mm(q_, s_), loc["intra"][ps_], f["rqw"][ps_],
                      st["ret"]), axis=1)
            st["ret"] = _each(lambda s_, d_, kv_: s_ * d_ + kv_, st["ret"], chunk_decay,
                              loc["kv"][ps_])

        def update_state(c):
            tmp["u"] = _each(lambda p_, u_: p_[:CHUNK] + u_, tmp["ps"], quads("u_loc", c))
            upd = _each(lambda u_, v_, bw_, kw_: _mm(jnp.concatenate([bw_, kw_], axis=0),
                                                     jnp.concatenate([u_.astype(BF16), v_], axis=0),
                                                     TN),
                        tmp["u"], quads("v", c), quads("bw", c), quads("kw", c))
            st["rw"] = _each(lambda s_, w_, d_: s_ * w_ + _mask_bd(d_, CHUNK),
                             st["rw"], quads("w_tot", c), upd)
            if c == n_chunks - 1:
                for qd in range(N_QUADS):
                    srw_ref[sq * N_QUADS + qd] = st["rw"][qd]
                for p in range(RET_PAIRS):
                    sret_ref[sq * RET_PAIRS + p] = st["ret"][p]

        def emit_y(c, ps, u):
            y = _each(lambda p_, a_, u_, b_: p_[CHUNK:] + b_ + _mm(a_, _bd_rows(u_, CHUNK)),
                      ps, quads("a_rb", c), u, quads("a_rk_v", c))
            res["y"][c] = jnp.concatenate(y, axis=1)

        stages = []
        for c in range(n_chunks):
            stages.append(functools.partial(read_state, c))
            stages.append(functools.partial(update_state, c))
            stages.append(lambda c=c: emit_y(c, tmp["ps"], tmp["u"]))
        return stages, res

    def output_stages(sq, x, f, res):
        tmp = {}
        blocks = []

        def gates():
            y = jnp.concatenate(res["y"], axis=0)
            inv_n = 1.0 / RWKV_HEAD_DIM
            yc = y - _head_sum(y) * inv_n
            var = _head_sum(yc * yc) * inv_n
            y = (yc * lax.rsqrt(var + LN_X_EPS) * ln_w + ln_b + f["bonus"]) * f["g"]
            o = jnp.concatenate(res["o"], axis=0)
            o = jnp.concatenate(
                [o[:, i * RET_V_DIM:(i + 1) * RET_V_DIM]
                 * lax.rsqrt(jnp.mean(jnp.square(o[:, i * RET_V_DIM:(i + 1) * RET_V_DIM]),
                                      axis=-1, keepdims=True) + NORM_EPS)
                 for i in range(RET_HEADS)], axis=1)
            rg = f["rg"]
            o = o * (rg * _sigmoid(rg))
            tmp["mixed"] = jnp.concatenate([y, o], axis=1).astype(BF16)

        def block(j):
            cs = slice(j * MXU_TILE, (j + 1) * MXU_TILE)
            blocks.append(x[:, cs] + _dot(tmp["mixed"], wout_ref[:, cs]))
            if j == D_MODEL // MXU_TILE - 1:
                o_ref[sq] = jnp.concatenate(blocks, axis=1)

        return [gates] + [functools.partial(block, j) for j in range(D_MODEL // MXU_TILE)]

    def proj_stages(h, out):
        def block(j):
            out.append(_dot(h, win_ref[:, j * MXU_TILE:(j + 1) * MXU_TILE]))
        return [functools.partial(block, j) for j in range(PROJ_WIDTH // MXU_TILE)]

    xs = [x_ref[sq] for sq in range(seqs)]
    hs = [_rms_norm(x, gain_ref[...]).astype(BF16) for x in xs]
    def interleave(a, b):
        out = []
        for i in range(max(len(a), len(b))):
            out += a[i:i + 1] + b[i:i + 1]
        return out

    p_blocks = [[] for _ in range(seqs)]
    fronts = [front_stages(sq, p_blocks[sq]) for sq in range(seqs)]
    _Fill(proj_stages(hs[0], p_blocks[0])).drain()
    _Fill(interleave(fronts[0][0], proj_stages(hs[1], p_blocks[1]) if seqs > 1 else [])).drain()
    state_todo, output_todo = [], []
    for sq in range(seqs):
        f = fronts[sq][1]
        ahead = fronts[sq + 1][0] if sq + 1 < seqs else []
        ahead = interleave(ahead, proj_stages(hs[sq + 2], p_blocks[sq + 2]) if sq + 2 < seqs else [])
        fill = _Fill(interleave(interleave(state_todo, output_todo), ahead), LOCAL_STAGES)
        loc = local(f, fill)
        fill.drain()
        stages, res = state_stages(sq, f, loc)
        output_todo = output_stages(sq - 1, xs[sq - 1], *done) if sq else []
        state_todo, done = stages, (f, res)
    _Fill(interleave(state_todo, output_todo)).drain()
    _Fill(output_stages(seqs - 1, xs[seqs - 1], *done)).drain()


def _ffn_kernel(*refs, final_norm, n_casts):
    x_ref, gain_ref, wg_ref, wu_ref, wd_ref, fin_ref = refs[:6]
    o_ref = refs[6 + n_casts]
    for src, dst in zip(refs[6:6 + n_casts], refs[7 + n_casts:]):
        dst[...] = src[...].astype(BF16)
    x = x_ref[...]
    h = _rms_norm(x, gain_ref[...]).astype(BF16)
    gate = _dot(h, wg_ref[...])
    up = _dot(h, wu_ref[...])
    act = (gate * _sigmoid(gate) * up).astype(BF16)
    y = x + 0.5 * _dot(act, wd_ref[...])
    if final_norm:
        y = _rms_norm(y, fin_ref[...])
    o_ref[...] = y


def _resident(shape):
    return pl.BlockSpec(shape, lambda *_: (0,) * len(shape), pipeline_mode=pl.Buffered(1))


def _layer(shape, l):
    return pl.BlockSpec((None,) + shape, lambda *_: (l,) + (0,) * len(shape),
                        pipeline_mode=pl.Buffered(1))


def _weight(w, l):
    return _layer(w.shape[1:], l) if w.ndim == 3 else _resident(w.shape)


def _ffn(x2, l, gain, wg, wu, wd, fin, final_norm, casts=()):
    t = x2.shape[0]
    rows = min(FFN_ROWS, t)
    steps = t // rows
    assert t % rows == 0 and CAST_ROWS % (16 * steps) == 0
    block = CAST_ROWS // steps
    views = [(w.reshape(w.shape[0], CAST_ROWS, -1), cl) for w, cl in casts]
    out = pl.pallas_call(
        functools.partial(_ffn_kernel, final_norm=final_norm, n_casts=len(views)),
        out_shape=[jax.ShapeDtypeStruct(x2.shape, F32)]
                  + [jax.ShapeDtypeStruct(w.shape[1:], BF16) for w, _ in views],
        grid=(steps,),
        in_specs=[pl.BlockSpec((rows, D_MODEL), lambda i: (i, 0)),
                  _layer((1, D_MODEL), l), _weight(wg, l), _weight(wu, l), _weight(wd, l),
                  _resident((1, D_MODEL))]
                 + [pl.BlockSpec((None, block, w.shape[2]), lambda i, cl=cl: (cl, i, 0))
                    for w, cl in views],
        out_specs=[pl.BlockSpec((rows, D_MODEL), lambda i: (i, 0))]
                  + [pl.BlockSpec((block, w.shape[2]), lambda i: (i, 0)) for w, _ in views],
        compiler_params=pltpu.CompilerParams(
            dimension_semantics=("arbitrary",), vmem_limit_bytes=VMEM_LIMIT_BYTES),
        name="ffn_final" if final_norm else "ffn",
    )(x2, gain, wg, wu, wd, fin, *[w for w, _ in views])
    return out[0], out[1:]


def _mixer_tables(seq, rows):
    idx = jnp.arange(rows)
    same_chunk = (idx[:, None] // CHUNK) == (idx[None, :] // CHUNK)
    tri = (same_chunk & (idx[None, :] <= idx[:, None])).astype(BF16)
    half = RET_QK_DIM // 2
    inv_freq = 1.0 / (ROPE_BASE ** jnp.linspace(0.0, 1.0, half, dtype=F32))
    ang = jnp.arange(seq, dtype=F32)[:, None] * inv_freq[None, :]
    cos = jnp.tile(jnp.cos(ang), (1, 2 * RET_HEADS))
    sin = jnp.tile(jnp.concatenate([-jnp.sin(ang), jnp.sin(ang)], axis=1), (1, RET_HEADS))
    log_gamma = jnp.log(1.0 - jnp.power(2.0, -5.0 - jnp.arange(RET_HEADS, dtype=F32)))
    pos = jnp.arange(CHUNK, dtype=F32)
    dist = jnp.abs(pos[:, None] - pos[None, :])
    dd = jnp.exp(log_gamma[:, None, None] * dist)
    dd = jnp.transpose(dd, (1, 0, 2)).reshape(CHUNK, RET_HEADS * CHUNK)
    query_w = jnp.exp(log_gamma[:, None] * (pos + 1.0)[None, :])
    key_w = jnp.exp(log_gamma[:, None] * (CHUNK - 1.0 - pos)[None, :])
    qw = jnp.tile(jnp.repeat(query_w.T, RET_QK_DIM, axis=1), (rows // CHUNK, 1))
    kw = jnp.tile(jnp.repeat(key_w.T, RET_QK_DIM, axis=1), (rows // CHUNK, 1))
    g64 = jnp.repeat(jnp.exp(log_gamma * CHUNK), RET_V_DIM)[None, :]
    return tri, cos, sin, dd, qw, kw, g64


def _mixer(x, l, gain, w_in, mu, lora, glora, vecs, w_out, tables):
    b, seq, _ = x.shape
    rows = min(MIX_ROWS, seq)
    seqs = MIX_SEQS if b % MIX_SEQS == 0 else 1
    assert seq % rows == 0 and rows % CHUNK == 0
    tri, cos, sin, dd, qw, kw, g64 = tables
    return pl.pallas_call(
        functools.partial(_mixer_kernel, seqs=seqs, rows=rows),
        out_shape=jax.ShapeDtypeStruct(x.shape, F32),
        grid=(b // seqs, seq // rows),
        in_specs=[pl.BlockSpec((seqs, rows, D_MODEL), lambda i, j: (i, j, 0)),
                  _layer((1, D_MODEL), l), _weight(w_in, l),
                  _layer((1, RWKV_IN), l), _layer((DECAY_LORA + ICL_LORA, 2 * RWKV_WIDTH), l),
                  _layer((GATE_LORA, RWKV_WIDTH), l), _layer((8, RWKV_WIDTH), l),
                  _weight(w_out, l),
                  _resident((rows, rows)),
                  pl.BlockSpec((rows, RET_QK_WIDTH), lambda i, j: (j, 0)),
                  pl.BlockSpec((rows, RET_QK_WIDTH), lambda i, j: (j, 0)),
                  _resident((CHUNK, RET_QK_WIDTH)), _resident((rows, RET_QK_WIDTH)),
                  _resident((rows, RET_QK_WIDTH)), _resident((1, RET_V_WIDTH))],
        out_specs=pl.BlockSpec((seqs, rows, D_MODEL), lambda i, j: (i, j, 0)),
        scratch_shapes=[pltpu.VMEM((seqs, 1, RWKV_IN), F32),
                        pltpu.VMEM((seqs * N_QUADS, QUAD, QUAD), F32),
                        pltpu.VMEM((seqs * RET_PAIRS, RET_QK_PAIR, RET_V_PAIR), F32)],
        compiler_params=pltpu.CompilerParams(
            dimension_semantics=("arbitrary", "arbitrary"), vmem_limit_bytes=VMEM_LIMIT_BYTES),
        name="mixer",
    )(x, gain, w_in, mu, lora, glora, vecs, w_out, tri, cos, sin, dd, qw, kw, g64)


def kernel(x, ffn1_norm, ffn1_w_gate, ffn1_w_up, ffn1_w_down, mix_norm, w_in, shift_mu, w0, w_lora_up, a0, a_lora_up, g_lora_up, k_k, k_a, r_k, ln_x_w, ln_x_b, w_out, ffn2_norm, ffn2_w_gate, ffn2_w_up, ffn2_w_down, final_norm):
    b, seq, d = x.shape
    depth = w_in.shape[0]
    tables = _mixer_tables(seq, min(MIX_ROWS, seq))
    bf = lambda z: z.astype(BF16)
    row = lambda z: z.reshape(depth, 1, -1)
    ffn1_w, ffn2_w = (ffn1_w_gate, ffn1_w_up, ffn1_w_down), (ffn2_w_gate, ffn2_w_up, ffn2_w_down)
    fin = final_norm[None, :]
    zeros = jnp.zeros((depth, DECAY_LORA, RWKV_WIDTH), F32)
    lora = bf(jnp.concatenate(
        [jnp.concatenate([w_lora_up, zeros], axis=2),
         jnp.concatenate([zeros, a_lora_up], axis=2)], axis=1))
    vecs = jnp.stack([w0, a0, k_k, k_a, r_k.reshape(depth, -1), ln_x_w, ln_x_b,
                      jnp.zeros((depth, RWKV_WIDTH), F32)], axis=1)
    glora = bf(g_lora_up)
    as_ffn = lambda ws: [w.reshape(s.shape[1:]) for w, s in zip(ws, ffn1_w)]
    w1 = [bf(w[0]) for w in ffn1_w]
    wm = [bf(w_in[0]), bf(w_out[0])]
    for l in range(depth):
        nxt = l + 1 < depth
        x, w2 = _ffn(x.reshape(b * seq, d), l, row(ffn1_norm), *w1, fin, False,
                     casts=[(w, l) for w in ffn2_w])
        x = _mixer(x.reshape(b, seq, d), l, row(mix_norm), wm[0], row(shift_mu), lora, glora,
                   vecs, wm[1], tables)
        x, cast = _ffn(x.reshape(b * seq, d), l, row(ffn2_norm), *as_ffn(w2), fin, not nxt,
                       casts=[(w, l + 1) for w in ffn1_w + (w_in, w_out)] if nxt else [])
        if nxt:
            w1 = as_ffn(cast[:3])
            wm = [cast[3].reshape(w_in.shape[1:]), cast[4].reshape(w_out.shape[1:])]
    return x.reshape(b, seq, d)
```

```python
import functools

import jax
import jax.numpy as jnp
from jax import lax
from jax.experimental import pallas as pl
from jax.experimental.pallas import tpu as pltpu

D_MODEL = 1024
D_FF = 2816
CHUNK = 64
RWKV_HEADS = 8
RWKV_HEAD_DIM = 64
RWKV_WIDTH = RWKV_HEADS * RWKV_HEAD_DIM
DECAY_LORA = 64
ICL_LORA = 64
GATE_LORA = 128
RWKV_IN = 3 * RWKV_WIDTH + DECAY_LORA + ICL_LORA + GATE_LORA
RET_HEADS = 4
RET_QK_DIM = 64
RET_V_DIM = 128
RET_QK_WIDTH = RET_HEADS * RET_QK_DIM
RET_V_WIDTH = RET_HEADS * RET_V_DIM
RET_IN = 2 * RET_QK_WIDTH + 2 * RET_V_WIDTH
PROJ_WIDTH = RWKV_IN + RET_IN
ROPE_BASE = 10000.0
NORM_EPS = 1e-6
LN_X_EPS = 64e-5

GROUP = 4
QUAD = GROUP * RWKV_HEAD_DIM
N_QUADS = RWKV_HEADS // GROUP
MXU_TILE = 256
RET_PAIRS = RET_HEADS // 2
RET_QK_PAIR = 2 * RET_QK_DIM
RET_V_PAIR = 2 * RET_V_DIM
FFN_ROWS = 1024
MIX_ROWS = 256
MIX_SEQS = 4
VMEM_LIMIT_BYTES = 58 * 1024 * 1024
BF16_ROWS = 16

F32 = jnp.float32
BF16 = jnp.bfloat16
NN = (((1,), (0,)), ((), ()))
TN = (((0,), (0,)), ((), ()))


def _dot(a, b, dims=NN):
    return lax.dot_general(a, b, dims, preferred_element_type=F32)


def _mm(a, b, dims=NN):
    return _dot(a.astype(BF16), b.astype(BF16), dims)


def _const_mm(c_bf16, x, terms):
    acc = None
    rem = x
    for _ in range(terms):
        piece = rem.astype(BF16)
        rem = rem - piece.astype(F32)
        d = _dot(c_bf16, piece)
        acc = d if acc is None else acc + d
    return acc


def _head_sum(z):
    lanes = 2 * RWKV_HEAD_DIM
    low = lax.broadcasted_iota(jnp.int32, (z.shape[0], lanes), 1) < RWKV_HEAD_DIM
    out = []
    for i in range(z.shape[1] // lanes):
        t = z[:, i * lanes:(i + 1) * lanes]
        s_lo = jnp.sum(jnp.where(low, t, 0.0), axis=-1, keepdims=True)
        s_hi = jnp.sum(jnp.where(low, 0.0, t), axis=-1, keepdims=True)
        out.append(jnp.where(low, s_lo, s_hi))
    return jnp.concatenate(out, axis=1)


def _sigmoid(x):
    return 1.0 / (1.0 + jnp.exp(-x))


def _softplus(x):
    return jnp.maximum(x, 0.0) + jnp.log(1.0 + jnp.exp(-jnp.abs(x)))


def _rms_norm(x, gain):
    return x * lax.rsqrt(jnp.mean(x * x, axis=-1, keepdims=True) + NORM_EPS) * gain


def _bd_rows(x, group_lanes):
    lane_head = lax.broadcasted_iota(jnp.int32, x.shape, 1) // group_lanes
    return jnp.concatenate(
        [jnp.where(lane_head == h, x, 0.0) for h in range(x.shape[1] // group_lanes)], axis=0)


def _mask_bd(z, group_lanes):
    lane_head = lax.broadcasted_iota(jnp.int32, (CHUNK, z.shape[1]), 1) // group_lanes
    return jnp.concatenate(
        [jnp.where(lane_head == h, z[h * CHUNK:(h + 1) * CHUNK], 0.0)
         for h in range(z.shape[1] // group_lanes)], axis=0)


def _chunk_rhs(z, n_heads):
    rows, width = z.shape
    n_chunks = rows // CHUNK
    zt = z.T
    rolled = [zt] + [pltpu.roll(zt, j * CHUNK, 1) for j in range(1, n_chunks)]
    lane_block = lax.broadcasted_iota(jnp.int32, (CHUNK, rows), 1) // CHUNK
    out = []
    for c in range(n_chunks):
        blocks = [jnp.where(lane_block == h,
                            rolled[(h - c) % n_chunks][h * CHUNK:(h + 1) * CHUNK], 0.0)[:, :width]
                  for h in range(n_heads)]
        out.append(jnp.concatenate(blocks, axis=0))
    return out


class _Fill:
    def __init__(self, pieces=(), calls=1):
        self.pieces = list(pieces)
        self.calls = calls

    def __call__(self):
        share = -(-len(self.pieces) // max(self.calls, 1))
        self.calls -= 1
        for _ in range(share):
            self.pieces.pop(0)()

    def drain(self):
        while self.pieces:
            self.pieces.pop(0)()


LOCAL_STAGES = 13
INV_BLOCK = 16


def _each(fn, *lists):
    return [fn(*args) for args in zip(*lists)]


def _prod(a_list, b_list, fill):
    out = _each(lambda a, b: _mm(a, _bd_rows(b, CHUNK)), a_list, b_list)
    fill()
    return out


def _add(a_list, b_list):
    return _each(lambda a, b: a + b, a_list, b_list)


def _tri_inverse(ns, fill):
    shape = ns[0].shape
    row = lax.broadcasted_iota(jnp.int32, shape, 0)
    col = lax.broadcasted_iota(jnp.int32, shape, 1) % CHUNK
    same16 = (row // INV_BLOCK) == (col // INV_BLOCK)
    same32 = (row // (2 * INV_BLOCK)) == (col // (2 * INV_BLOCK))
    n0 = [jnp.where(same16, n, 0.0) for n in ns]
    n1 = [jnp.where(same32, jnp.where(same16, 0.0, n), 0.0) for n in ns]
    n2 = [jnp.where(same32, 0.0, n) for n in ns]
    eye = jnp.where(row == col, 1.0, 0.0)
    t = [eye + n for n in n0]
    m = _prod(n0, n0, fill)
    for _ in range(2):
        tm = _prod(_each(lambda a, b: jnp.concatenate([a, b], axis=0), t, m), m, fill)
        t = _each(lambda a, b: a + b[:CHUNK], t, tm)
        m = [x[CHUNK:] for x in tm]
    t = _add(t, _prod(t, m, fill))
    stack = lambda a, b: jnp.concatenate([a, b], axis=0)
    nt = _prod(_each(stack, n1, n2), t, fill)
    x1 = [x[:CHUNK] for x in nt]
    tz = _prod(_each(lambda a, b: stack(a, b[CHUNK:]), t, nt), x1, fill)
    t = _each(lambda a, b: a + b[:CHUNK], t, tz)
    n2t = _each(lambda a, b: a[CHUNK:] + b[CHUNK:], nt, tz)
    return _add(t, _prod(t, n2t, fill))


def _rwkv_local(at, rt, wb, wk, v, fill):
    a = _each(lambda at_, rt_, wb_, wk_: _mm(
        jnp.concatenate([at_, rt_], axis=0), jnp.concatenate([wb_, wk_], axis=1)),
        at, rt, wb, wk)
    fill()
    row = lax.broadcasted_iota(jnp.int32, (CHUNK, 2 * QUAD), 0)
    col = lax.broadcasted_iota(jnp.int32, (CHUNK, 2 * QUAD), 1) % CHUNK
    a_a = [jnp.where(col < row, x[:CHUNK], 0.0) for x in a]
    a_r = [jnp.where(col <= row, x[CHUNK:], 0.0) for x in a]
    av = _each(lambda x, y, v_: _mm(jnp.concatenate([x[:, QUAD:], y[:, QUAD:]], axis=0),
                                    _bd_rows(v_, CHUNK)), a_a, a_r, v)
    fill()
    t = _tri_inverse([x[:, :QUAD] for x in a_a], fill)
    tu = _each(lambda t_, at_, av_: _mm(
        t_, jnp.concatenate([_bd_rows(at_, CHUNK),
                             _bd_rows(av_[:CHUNK].astype(BF16), CHUNK)], axis=1)),
        t, at, av)
    fill()
    return ([x[:, :QUAD] for x in tu], [x[:, QUAD:] for x in tu],
            [x[:, :QUAD] for x in a_r], [x[CHUNK:] for x in av])


def _ret_local(q, k, k_w, v, intra_decay, fill):
    scores = _each(lambda q_, k_, d_: _mm(q_, k_) * d_, q, k, intra_decay)
    fill()
    intra = _each(lambda s_, v_: _mm(s_, _bd_rows(v_, RET_V_DIM)), scores, v)
    fill()
    kv = _each(lambda k_, v_: _mask_bd(_mm(k_, v_, TN), RET_V_DIM), k_w, v)
    fill()
    return intra, kv


def _rope(x, cos, sin_signed):
    half = RET_QK_DIM // 2
    width = x.shape[1]
    lane = lax.broadcasted_iota(jnp.int32, x.shape, 1) % RET_QK_DIM
    partner = jnp.where(lane < half, pltpu.roll(x, width - half, 1), pltpu.roll(x, half, 1))
    return x * cos + partner * sin_signed


def _mixer_kernel(x_ref, gain_ref, win_ref, mu_ref, lora_ref, glora_ref, vecs_ref, wout_ref,
                  tri_ref, cos_ref, sin_ref, dd_ref, qw_ref, kw_ref, g64_ref,
                  o_ref, carry_ref, srw_ref, sret_ref, *, seqs, rows):
    @pl.when(pl.program_id(1) == 0)
    def _():
        carry_ref[...] = jnp.zeros_like(carry_ref)
        srw_ref[...] = jnp.zeros_like(srw_ref)
        sret_ref[...] = jnp.zeros_like(sret_ref)

    n_chunks = rows // CHUNK
    w = RWKV_WIDTH
    vecs = vecs_ref[...]
    w0, a0, k_k, k_a, r_k, ln_w, ln_b = (vecs[i:i + 1] for i in range(7))
    intra_decay = [dd_ref[:, p * RET_QK_PAIR:(p + 1) * RET_QK_PAIR] for p in range(RET_PAIRS)]
    chunk_decay = [g64_ref[:, p * RET_V_PAIR:(p + 1) * RET_V_PAIR] for p in range(RET_PAIRS)]

    def pieces(z):
        return [z[c * CHUNK:(c + 1) * CHUNK, qd * QUAD:(qd + 1) * QUAD]
                for c in range(n_chunks) for qd in range(N_QUADS)]

    def pairs(z, lanes):
        return [z[c * CHUNK:(c + 1) * CHUNK, p * lanes:(p + 1) * lanes]
                for c in range(n_chunks) for p in range(RET_PAIRS)]

    def front_stages(sq, p_blocks):
        t, f = {}, {}
        q0 = RWKV_IN

        def shift():
            p = t["p"] = jnp.concatenate(p_blocks, axis=1)
            pr = p[:, :RWKV_IN]
            first = lax.broadcasted_iota(jnp.int32, pr.shape, 0) == 0
            prev = jnp.where(first, carry_ref[sq], pltpu.roll(pr, 1, 0))
            carry_ref[sq] = pr[rows - 1:rows, :]
            t["ps"] = pr + mu_ref[...] * (prev - pr)

        def loras():
            ps = t["ps"]
            wa = ps[:, 3 * w:3 * w + DECAY_LORA + ICL_LORA]
            gd = ps[:, 3 * w + DECAY_LORA + ICL_LORA:]
            is_decay = lax.broadcasted_iota(jnp.int32, wa.shape, 1) < DECAY_LORA
            lora = _dot(jnp.where(is_decay, jnp.tanh(wa), wa).astype(BF16), lora_ref[...])
            log_w = -_softplus(-(w0 + lora[:, :w])) - 0.5
            t["log_decay"] = -jnp.exp(log_w)
            t["a"] = _sigmoid(a0 + lora[:, w:])
            f["g"] = _dot(_sigmoid(gd).astype(BF16), glora_ref[...])

        def keys():
            ps = t["ps"]
            r, k, v = ps[:, :w], ps[:, w:2 * w], ps[:, 2 * w:3 * w]
            kk = k * k_k
            t["kk"] = kk * lax.rsqrt(jnp.maximum(_head_sum(kk * kk), 1e-24))
            t["k_mod"] = k * (1.0 + (t["a"] - 1.0) * k_a)
            f["bonus"] = _head_sum(r * t["k_mod"] * r_k) * v
            f["v"] = pieces(v.astype(BF16))

        def decays():
            cum = _const_mm(tri_ref[...], t["log_decay"], 2)
            tot = jnp.concatenate(
                [jnp.broadcast_to(cum[(i + 1) * CHUNK - 1:(i + 1) * CHUNK], (CHUNK, w))
                 for i in range(n_chunks)], axis=0)
            t["cum"], t["tot"] = cum, tot
            tot_t = [jnp.exp(tot[:, qd * QUAD:(qd + 1) * QUAD]).T for qd in range(N_QUADS)]
            f["w_tot"] = [tot_t[qd][:, c * CHUNK:c * CHUNK + 1]
                          for c in range(n_chunks) for qd in range(N_QUADS)]
            f["at"] = pieces((-t["kk"] * jnp.exp(cum - t["log_decay"])).astype(BF16))
            f["rt"] = pieces((t["ps"][:, :w] * jnp.exp(cum)).astype(BF16))

        def scaled():
            kka = t["kk"] * t["a"]
            w_inv = jnp.exp(-t["cum"])
            w_end = jnp.exp(t["tot"] - t["cum"])
            scale = lambda z, s: pieces((z * s).astype(BF16))
            f["bw"], f["kw"] = scale(kka, w_end), scale(t["k_mod"], w_end)

            def rhs(z):
                per_quad = [_chunk_rhs(z[:, qd * QUAD:(qd + 1) * QUAD], GROUP)
                            for qd in range(N_QUADS)]
                return [per_quad[qd][c] for c in range(n_chunks) for qd in range(N_QUADS)]
            f["wb"], f["wk"] = rhs(kka * w_inv), rhs(t["k_mod"] * w_inv)

        def retention():
            p = t["p"]
            cos = cos_ref[...]
            sin = sin_ref[...]
            rq = _rope(p[:, q0:q0 + RET_QK_WIDTH], cos, sin)
            rk = _rope(p[:, q0 + RET_QK_WIDTH:q0 + 2 * RET_QK_WIDTH], cos, sin) \
                * (RET_QK_DIM ** -0.5)
            qk = lambda z: pairs(z.astype(BF16), RET_QK_PAIR)
            f["rq"] = qk(rq)
            per_pair = [_chunk_rhs(rk[:, p_ * RET_QK_PAIR:(p_ + 1) * RET_QK_PAIR], 2)
                        for p_ in range(RET_PAIRS)]
            f["rk"] = [per_pair[p_][c] for c in range(n_chunks) for p_ in range(RET_PAIRS)]
            f["rqw"], f["rkw"] = qk(rq * qw_ref[...]), qk(rk * kw_ref[...])
            f["rv"] = pairs(p[:, q0 + 2 * RET_QK_WIDTH:q0 + 2 * RET_QK_WIDTH + RET_V_WIDTH]
                            .astype(BF16), RET_V_PAIR)
            f["rg"] = p[:, q0 + 2 * RET_QK_WIDTH + RET_V_WIDTH:]

        return [shift, loras, keys, decays, scaled, retention], f

    def local(f, fill):
        a_hat, u_loc, a_rb, a_rk_v = _rwkv_local(f["at"], f["rt"], f["wb"], f["wk"], f["v"], fill)
        intra, kv = _ret_local(f["rq"], f["rk"], f["rkw"], f["rv"], intra_decay * n_chunks, fill)
        return dict(a_hat=a_hat, u_loc=u_loc, a_rb=a_rb, a_rk_v=a_rk_v, intra=intra, kv=kv)

    def state_stages(sq, f, loc):
        st = dict(rw=[srw_ref[sq * N_QUADS + qd] for qd in range(N_QUADS)],
                  ret=[sret_ref[sq * RET_PAIRS + p] for p in range(RET_PAIRS)])
        res = dict(y=[None] * n_chunks, o=[None] * n_chunks)
        tmp = {}

        def quads(name, c):
            return f[name][c * N_QUADS:(c + 1) * N_QUADS] if name in f else \
                loc[name][c * N_QUADS:(c + 1) * N_QUADS]

        def read_state(c):
            tmp["ps"] = _each(lambda a_, r_, s_: _mm(
                jnp.concatenate([a_.astype(BF16), r_], axis=0), s_),
                              quads("a_hat", c), quads("rt", c), st["rw"])
            ps_ = slice(c * RET_PAIRS, (c + 1) * RET_PAIRS)
            res["o"][c] = jnp.concatenate(
                _each(lambda i_, q_, s_: i_ + _mm(q_, s_), loc["intra"][ps_], f["rqw"][ps_],
                      st["ret"]), axis=1)
            st["ret"] = _each(lambda s_, d_, kv_: s_ * d_ + kv_, st["ret"], chunk_decay,
                              loc["kv"][ps_])

        def update_state(c):
            tmp["u"] = _each(lambda p_, u_: p_[:CHUNK] + u_, tmp["ps"], quads("u_loc", c))
            upd = _each(lambda u_, v_, bw_, kw_: _mm(jnp.concatenate([bw_, kw_], axis=0),
                                                     jnp.concatenate([u_.astype(BF16), v_], axis=0),
                                                     TN),
                        tmp["u"], quads("v", c), quads("bw", c), quads("kw", c))
            st["rw"] = _each(lambda s_, w_, d_: s_ * w_ + _mask_bd(d_, CHUNK),
                             st["rw"], quads("w_tot", c), upd)
            if c == n_chunks - 1:
                for qd in range(N_QUADS):
                    srw_ref[sq * N_QUADS + qd] = st["rw"][qd]
                for p in range(RET_PAIRS):
                    sret_ref[sq * RET_PAIRS + p] = st["ret"][p]

        def emit_y(c, ps, u):
            y = _each(lambda p_, a_, u_, b_: p_[CHUNK:] + b_ + _mm(a_, _bd_rows(u_, CHUNK)),
                      ps, quads("a_rb", c), u, quads("a_rk_v", c))
            res["y"][c] = jnp.concatenate(y, axis=1)

        stages = []
        for c in range(n_chunks):
            stages.append(functools.partial(read_state, c))
            stages.append(functools.partial(update_state, c))
            stages.append(lambda c=c: emit_y(c, tmp["ps"], tmp["u"]))
        return stages, res

    def output_stages(sq, x, f, res):
        tmp = {}
        blocks = []

        def gates():
            y = jnp.concatenate(res["y"], axis=0)
            inv_n = 1.0 / RWKV_HEAD_DIM
            yc = y - _head_sum(y) * inv_n
            var = _head_sum(yc * yc) * inv_n
            y = (yc * lax.rsqrt(var + LN_X_EPS) * ln_w + ln_b + f["bonus"]) * f["g"]
            o = jnp.concatenate(res["o"], axis=0)
            o = jnp.concatenate(
                [o[:, i * RET_V_DIM:(i + 1) * RET_V_DIM]
                 * lax.rsqrt(jnp.mean(jnp.square(o[:, i * RET_V_DIM:(i + 1) * RET_V_DIM]),
                                      axis=-1, keepdims=True) + NORM_EPS)
                 for i in range(RET_HEADS)], axis=1)
            rg = f["rg"]
            o = o * (rg * _sigmoid(rg))
            tmp["mixed"] = jnp.concatenate([y, o], axis=1).astype(BF16)

        def block(j):
            cs = slice(j * MXU_TILE, (j + 1) * MXU_TILE)
            blocks.append(x[:, cs] + _dot(tmp["mixed"], wout_ref[:, cs]))
            if j == D_MODEL // MXU_TILE - 1:
                o_ref[sq] = jnp.concatenate(blocks, axis=1)

        return [gates] + [functools.partial(block, j) for j in range(D_MODEL // MXU_TILE)]

    def proj_stages(h, out):
        def block(j):
            out.append(_dot(h, win_ref[:, j * MXU_TILE:(j + 1) * MXU_TILE]))
        return [functools.partial(block, j) for j in range(PROJ_WIDTH // MXU_TILE)]

    xs = [x_ref[sq] for sq in range(seqs)]
    hs = [_rms_norm(x, gain_ref[...]).astype(BF16) for x in xs]
    def interleave(a, b):
        out = []
        for i in range(max(len(a), len(b))):
            out += a[i:i + 1] + b[i:i + 1]
        return out

    p_blocks = [[] for _ in range(seqs)]
    fronts = [front_stages(sq, p_blocks[sq]) for sq in range(seqs)]
    _Fill(proj_stages(hs[0], p_blocks[0])).drain()
    _Fill(interleave(fronts[0][0], proj_stages(hs[1], p_blocks[1]) if seqs > 1 else [])).drain()
    state_todo, output_todo = [], []
    for sq in range(seqs):
        f = fronts[sq][1]
        ahead = fronts[sq + 1][0] if sq + 1 < seqs else []
        ahead = interleave(ahead, proj_stages(hs[sq + 2], p_blocks[sq + 2]) if sq + 2 < seqs else [])
        fill = _Fill(interleave(interleave(state_todo, output_todo), ahead), LOCAL_STAGES)
        loc = local(f, fill)
        fill.drain()
        stages, res = state_stages(sq, f, loc)
        output_todo = output_stages(sq - 1, xs[sq - 1], *done) if sq else []
        state_todo, done = stages, (f, res)
    _Fill(interleave(state_todo, output_todo)).drain()
    _Fill(output_stages(seqs - 1, xs[seqs - 1], *done)).drain()


def _ffn_kernel(*refs, final_norm, n_casts):
    x_ref, gain_ref, wg_ref, wu_ref, wd_ref, fin_ref = refs[:6]
    o_ref = refs[6 + n_casts]
    for src, dst in zip(refs[6:6 + n_casts], refs[7 + n_casts:]):
        dst[...] = src[...].astype(BF16)
    x = x_ref[...]
    h = _rms_norm(x, gain_ref[...]).astype(BF16)
    gate = _dot(h, wg_ref[...])
    up = _dot(h, wu_ref[...])
    act = (gate * _sigmoid(gate) * up).astype(BF16)
    y = x + 0.5 * _dot(act, wd_ref[...])
    if final_norm:
        y = _rms_norm(y, fin_ref[...])
    o_ref[...] = y


def _resident(shape):
    return pl.BlockSpec(shape, lambda *_: (0,) * len(shape), pipeline_mode=pl.Buffered(1))


def _layer(shape, l):
    return pl.BlockSpec((None,) + shape, lambda *_: (l,) + (0,) * len(shape),
                        pipeline_mode=pl.Buffered(1))


def _weight(w, l):
    return _layer(w.shape[1:], l) if w.ndim == 3 else _resident(w.shape)


def _ffn(x2, l, gain, wg, wu, wd, fin, final_norm, casts=()):
    t = x2.shape[0]
    rows = min(FFN_ROWS, t)
    steps = t // rows
    assert t % rows == 0

    def row_blocks(n_rows):
        return max(n for n in range(1, steps + 1)
                   if steps % n == 0 and n_rows % (n * BF16_ROWS) == 0)

    views = [(w, cl, row_blocks(w.shape[1])) for w, cl in casts]
    out = pl.pallas_call(
        functools.partial(_ffn_kernel, final_norm=final_norm, n_casts=len(views)),
        out_shape=[jax.ShapeDtypeStruct(x2.shape, F32)]
                  + [jax.ShapeDtypeStruct(w.shape[1:], BF16) for w, _, _ in views],
        grid=(steps,),
        in_specs=[pl.BlockSpec((rows, D_MODEL), lambda i: (i, 0)),
                  _layer((1, D_MODEL), l), _weight(wg, l), _weight(wu, l), _weight(wd, l),
                  _resident((1, D_MODEL))]
                 + [pl.BlockSpec((None, w.shape[1] // n, w.shape[2]),
                                 lambda i, cl=cl, n=n: (cl, i * n // steps, 0))
                    for w, cl, n in views],
        out_specs=[pl.BlockSpec((rows, D_MODEL), lambda i: (i, 0))]
                  + [pl.BlockSpec((w.shape[1] // n, w.shape[2]), lambda i, n=n: (i * n // steps, 0))
                     for w, _, n in views],
        compiler_params=pltpu.CompilerParams(
            dimension_semantics=("arbitrary",), vmem_limit_bytes=VMEM_LIMIT_BYTES),
        name="ffn_final" if final_norm else "ffn",
    )(x2, gain, wg, wu, wd, fin, *[w for w, _, _ in views])
    return out[0], list(out[1:])


def _mixer_tables(seq, rows):
    idx = jnp.arange(rows)
    same_chunk = (idx[:, None] // CHUNK) == (idx[None, :] // CHUNK)
    tri = (same_chunk & (idx[None, :] <= idx[:, None])).astype(BF16)
    half = RET_QK_DIM // 2
    inv_freq = 1.0 / (ROPE_BASE ** jnp.linspace(0.0, 1.0, half, dtype=F32))
    ang = jnp.arange(seq, dtype=F32)[:, None] * inv_freq[None, :]
    cos = jnp.tile(jnp.cos(ang), (1, 2 * RET_HEADS))
    sin = jnp.tile(jnp.concatenate([-jnp.sin(ang), jnp.sin(ang)], axis=1), (1, RET_HEADS))
    log_gamma = jnp.log(1.0 - jnp.power(2.0, -5.0 - jnp.arange(RET_HEADS, dtype=F32)))
    pos = jnp.arange(CHUNK, dtype=F32)
    dist = jnp.abs(pos[:, None] - pos[None, :])
    dd = jnp.exp(log_gamma[:, None, None] * dist)
    dd = jnp.transpose(dd, (1, 0, 2)).reshape(CHUNK, RET_HEADS * CHUNK)
    query_w = jnp.exp(log_gamma[:, None] * (pos + 1.0)[None, :])
    key_w = jnp.exp(log_gamma[:, None] * (CHUNK - 1.0 - pos)[None, :])
    qw = jnp.tile(jnp.repeat(query_w.T, RET_QK_DIM, axis=1), (rows // CHUNK, 1))
    kw = jnp.tile(jnp.repeat(key_w.T, RET_QK_DIM, axis=1), (rows // CHUNK, 1))
    g64 = jnp.repeat(jnp.exp(log_gamma * CHUNK), RET_V_DIM)[None, :]
    return tri, cos, sin, dd, qw, kw, g64


def _mixer(x, l, gain, w_in, mu, lora, glora, vecs, w_out, tables):
    b, seq, _ = x.shape
    rows = min(MIX_ROWS, seq)
    seqs = MIX_SEQS if b % MIX_SEQS == 0 else 1
    assert seq % rows == 0 and rows % CHUNK == 0
    tri, cos, sin, dd, qw, kw, g64 = tables
    return pl.pallas_call(
        functools.partial(_mixer_kernel, seqs=seqs, rows=rows),
        out_shape=jax.ShapeDtypeStruct(x.shape, F32),
        grid=(b // seqs, seq // rows),
        in_specs=[pl.BlockSpec((seqs, rows, D_MODEL), lambda i, j: (i, j, 0)),
                  _layer((1, D_MODEL), l), _weight(w_in, l),
                  _layer((1, RWKV_IN), l), _layer((DECAY_LORA + ICL_LORA, 2 * RWKV_WIDTH), l),
                  _layer((GATE_LORA, RWKV_WIDTH), l), _layer((8, RWKV_WIDTH), l),
                  _weight(w_out, l),
                  _resident((rows, rows)),
                  pl.BlockSpec((rows, RET_QK_WIDTH), lambda i, j: (j, 0)),
                  pl.BlockSpec((rows, RET_QK_WIDTH), lambda i, j: (j, 0)),
                  _resident((CHUNK, RET_QK_WIDTH)), _resident((rows, RET_QK_WIDTH)),
                  _resident((rows, RET_QK_WIDTH)), _resident((1, RET_V_WIDTH))],
        out_specs=pl.BlockSpec((seqs, rows, D_MODEL), lambda i, j: (i, j, 0)),
        scratch_shapes=[pltpu.VMEM((seqs, 1, RWKV_IN), F32),
                        pltpu.VMEM((seqs * N_QUADS, QUAD, QUAD), F32),
                        pltpu.VMEM((seqs * RET_PAIRS, RET_QK_PAIR, RET_V_PAIR), F32)],
        compiler_params=pltpu.CompilerParams(
            dimension_semantics=("arbitrary", "arbitrary"), vmem_limit_bytes=VMEM_LIMIT_BYTES),
        name="mixer",
    )(x, gain, w_in, mu, lora, glora, vecs, w_out, tri, cos, sin, dd, qw, kw, g64)


def kernel(x, ffn1_norm, ffn1_w_gate, ffn1_w_up, ffn1_w_down, mix_norm, w_in, shift_mu, w0, w_lora_up, a0, a_lora_up, g_lora_up, k_k, k_a, r_k, ln_x_w, ln_x_b, w_out, ffn2_norm, ffn2_w_gate, ffn2_w_up, ffn2_w_down, final_norm):
    b, seq, d = x.shape
    depth = w_in.shape[0]
    tables = _mixer_tables(seq, min(MIX_ROWS, seq))
    bf = lambda z: z.astype(BF16)
    row = lambda z: z.reshape(depth, 1, -1)
    ffn1_w, ffn2_w = (ffn1_w_gate, ffn1_w_up, ffn1_w_down), (ffn2_w_gate, ffn2_w_up, ffn2_w_down)
    fin = final_norm[None, :]
    zeros = jnp.zeros((depth, DECAY_LORA, RWKV_WIDTH), F32)
    lora = bf(jnp.concatenate(
        [jnp.concatenate([w_lora_up, zeros], axis=2),
         jnp.concatenate([zeros, a_lora_up], axis=2)], axis=1))
    vecs = jnp.stack([w0, a0, k_k, k_a, r_k.reshape(depth, -1), ln_x_w, ln_x_b,
                      jnp.zeros((depth, RWKV_WIDTH), F32)], axis=1)
    glora = bf(g_lora_up)
    w1 = [bf(w[0]) for w in ffn1_w]
    wm = [bf(w_in[0]), bf(w_out[0])]
    for l in range(depth):
        nxt = l + 1 < depth
        x, w2 = _ffn(x.reshape(b * seq, d), l, row(ffn1_norm), *w1, fin, False,
                     casts=[(w, l) for w in ffn2_w])
        x = _mixer(x.reshape(b, seq, d), l, row(mix_norm), wm[0], row(shift_mu), lora, glora,
                   vecs, wm[1], tables)
        x, cast = _ffn(x.reshape(b * seq, d), l, row(ffn2_norm), *w2, fin, not nxt,
                       casts=[(w, l + 1) for w in ffn1_w + (w_in, w_out)] if nxt else [])
        if nxt:
            w1, wm = cast[:3], cast[3:]
    return x.reshape(b, seq, d)
```

```python
import functools

import jax
import jax.numpy as jnp
from jax import lax
from jax.experimental import pallas as pl
from jax.experimental.pallas import tpu as pltpu

D_MODEL = 1024
D_FF = 2816
CHUNK = 64
RWKV_HEADS = 8
RWKV_HEAD_DIM = 64
RWKV_WIDTH = RWKV_HEADS * RWKV_HEAD_DIM
DECAY_LORA = 64
ICL_LORA = 64
GATE_LORA = 128
RWKV_IN = 3 * RWKV_WIDTH + DECAY_LORA + ICL_LORA + GATE_LORA
RET_HEADS = 4
RET_QK_DIM = 64
RET_V_DIM = 128
RET_QK_WIDTH = RET_HEADS * RET_QK_DIM
RET_V_WIDTH = RET_HEADS * RET_V_DIM
RET_IN = 2 * RET_QK_WIDTH + 2 * RET_V_WIDTH
PROJ_WIDTH = RWKV_IN + RET_IN
ROPE_BASE = 10000.0
NORM_EPS = 1e-6
LN_X_EPS = 64e-5

GROUP = 4
QUAD = GROUP * RWKV_HEAD_DIM
N_QUADS = RWKV_HEADS // GROUP
MXU_TILE = 256
RET_PAIRS = RET_HEADS // 2
RET_QK_PAIR = 2 * RET_QK_DIM
RET_V_PAIR = 2 * RET_V_DIM
FFN_ROWS = 1024
MIX_ROWS = 256
MIX_SEQS = 4
VMEM_LIMIT_BYTES = 58 * 1024 * 1024
BF16_ROWS = 16

F32 = jnp.float32
BF16 = jnp.bfloat16
NN = (((1,), (0,)), ((), ()))
TN = (((0,), (0,)), ((), ()))


def _dot(a, b, dims=NN):
    return lax.dot_general(a, b, dims, preferred_element_type=F32)


def _mm(a, b, dims=NN):
    return _dot(a.astype(BF16), b.astype(BF16), dims)


def _const_mm(c_bf16, x, terms):
    acc = None
    rem = x
    for _ in range(terms):
        piece = rem.astype(BF16)
        rem = rem - piece.astype(F32)
        d = _dot(c_bf16, piece)
        acc = d if acc is None else acc + d
    return acc


def _head_sum(z):
    lanes = 2 * RWKV_HEAD_DIM
    low = lax.broadcasted_iota(jnp.int32, (z.shape[0], lanes), 1) < RWKV_HEAD_DIM
    out = []
    for i in range(z.shape[1] // lanes):
        t = z[:, i * lanes:(i + 1) * lanes]
        s_lo = jnp.sum(jnp.where(low, t, 0.0), axis=-1, keepdims=True)
        s_hi = jnp.sum(jnp.where(low, 0.0, t), axis=-1, keepdims=True)
        out.append(jnp.where(low, s_lo, s_hi))
    return jnp.concatenate(out, axis=1)


def _sigmoid(x):
    return 1.0 / (1.0 + jnp.exp(-x))


def _softplus(x):
    return jnp.maximum(x, 0.0) + jnp.log(1.0 + jnp.exp(-jnp.abs(x)))


def _rms_norm(x, gain):
    return x * lax.rsqrt(jnp.mean(x * x, axis=-1, keepdims=True) + NORM_EPS) * gain


def _bd_rows(x, group_lanes):
    lane_head = lax.broadcasted_iota(jnp.int32, x.shape, 1) // group_lanes
    return jnp.concatenate(
        [jnp.where(lane_head == h, x, 0.0) for h in range(x.shape[1] // group_lanes)], axis=0)


def _mask_bd(z, group_lanes):
    lane_head = lax.broadcasted_iota(jnp.int32, (CHUNK, z.shape[1]), 1) // group_lanes
    return jnp.concatenate(
        [jnp.where(lane_head == h, z[h * CHUNK:(h + 1) * CHUNK], 0.0)
         for h in range(z.shape[1] // group_lanes)], axis=0)


def _chunk_rhs(z, n_heads):
    rows, width = z.shape
    n_chunks = rows // CHUNK
    zt = z.T
    rolled = [zt] + [pltpu.roll(zt, j * CHUNK, 1) for j in range(1, n_chunks)]
    lane_block = lax.broadcasted_iota(jnp.int32, (CHUNK, rows), 1) // CHUNK
    out = []
    for c in range(n_chunks):
        blocks = [jnp.where(lane_block == h,
                            rolled[(h - c) % n_chunks][h * CHUNK:(h + 1) * CHUNK], 0.0)[:, :width]
                  for h in range(n_heads)]
        out.append(jnp.concatenate(blocks, axis=0))
    return out


class _Fill:
    def __init__(self, pieces=(), calls=1):
        self.pieces = list(pieces)
        self.calls = calls

    def __call__(self):
        share = -(-len(self.pieces) // max(self.calls, 1))
        self.calls -= 1
        for _ in range(share):
            self.pieces.pop(0)()

    def drain(self):
        while self.pieces:
            self.pieces.pop(0)()


LOCAL_STAGES = 13
INV_BLOCK = 16


def _each(fn, *lists):
    return [fn(*args) for args in zip(*lists)]


def _prod(a_list, b_list, fill):
    out = _each(lambda a, b: _mm(a, _bd_rows(b, CHUNK)), a_list, b_list)
    fill()
    return out


def _add(a_list, b_list):
    return _each(lambda a, b: a + b, a_list, b_list)


def _tri_inverse(ns, fill):
    shape = ns[0].shape
    row = lax.broadcasted_iota(jnp.int32, shape, 0)
    col = lax.broadcasted_iota(jnp.int32, shape, 1) % CHUNK
    same16 = (row // INV_BLOCK) == (col // INV_BLOCK)
    same32 = (row // (2 * INV_BLOCK)) == (col // (2 * INV_BLOCK))
    n0 = [jnp.where(same16, n, 0.0) for n in ns]
    n1 = [jnp.where(same32, jnp.where(same16, 0.0, n), 0.0) for n in ns]
    n2 = [jnp.where(same32, 0.0, n) for n in ns]
    eye = jnp.where(row == col, 1.0, 0.0)
    t = [eye + n for n in n0]
    m = _prod(n0, n0, fill)
    for _ in range(2):
        tm = _prod(_each(lambda a, b: jnp.concatenate([a, b], axis=0), t, m), m, fill)
        t = _each(lambda a, b: a + b[:CHUNK], t, tm)
        m = [x[CHUNK:] for x in tm]
    t = _add(t, _prod(t, m, fill))
    stack = lambda a, b: jnp.concatenate([a, b], axis=0)
    nt = _prod(_each(stack, n1, n2), t, fill)
    x1 = [x[:CHUNK] for x in nt]
    tz = _prod(_each(lambda a, b: stack(a, b[CHUNK:]), t, nt), x1, fill)
    t = _each(lambda a, b: a + b[:CHUNK], t, tz)
    n2t = _each(lambda a, b: a[CHUNK:] + b[CHUNK:], nt, tz)
    return _add(t, _prod(t, n2t, fill))


def _rwkv_local(at, rt, wb, wk, v, fill):
    a = _each(lambda at_, rt_, wb_, wk_: _mm(
        jnp.concatenate([at_, rt_], axis=0), jnp.concatenate([wb_, wk_], axis=1)),
        at, rt, wb, wk)
    fill()
    row = lax.broadcasted_iota(jnp.int32, (CHUNK, 2 * QUAD), 0)
    col = lax.broadcasted_iota(jnp.int32, (CHUNK, 2 * QUAD), 1) % CHUNK
    a_a = [jnp.where(col < row, x[:CHUNK], 0.0) for x in a]
    a_r = [jnp.where(col <= row, x[CHUNK:], 0.0) for x in a]
    av = _each(lambda x, y, v_: _mm(jnp.concatenate([x[:, QUAD:], y[:, QUAD:]], axis=0),
                                    _bd_rows(v_, CHUNK)), a_a, a_r, v)
    fill()
    t = _tri_inverse([x[:, :QUAD] for x in a_a], fill)
    tu = _each(lambda t_, at_, av_: _mm(
        t_, jnp.concatenate([_bd_rows(at_, CHUNK),
                             _bd_rows(av_[:CHUNK].astype(BF16), CHUNK)], axis=1)),
        t, at, av)
    fill()
    return ([x[:, :QUAD] for x in tu], [x[:, QUAD:] for x in tu],
            [x[:, :QUAD] for x in a_r], [x[CHUNK:] for x in av])


def _ret_local(q, k, k_w, v, intra_decay, fill):
    scores = _each(lambda q_, k_, d_: _mm(q_, k_) * d_, q, k, intra_decay)
    fill()
    intra = _each(lambda s_, v_: _mm(s_, _bd_rows(v_, RET_V_DIM)), scores, v)
    fill()
    kv = _each(lambda k_, v_: _mask_bd(_mm(k_, v_, TN), RET_V_DIM), k_w, v)
    fill()
    return intra, kv


def _rope(x, cos, sin_signed):
    half = RET_QK_DIM // 2
    width = x.shape[1]
    lane = lax.broadcasted_iota(jnp.int32, x.shape, 1) % RET_QK_DIM
    partner = jnp.where(lane < half, pltpu.roll(x, width - half, 1), pltpu.roll(x, half, 1))
    return x * cos + partner * sin_signed


def _mixer_kernel(x_ref, gain_ref, win_ref, mu_ref, lora_ref, glora_ref, vecs_ref, wout_ref,
                  tri_ref, cos_ref, sin_ref, dd_ref, qw_ref, kw_ref, g64_ref,
                  o_ref, carry_ref, srw_ref, sret_ref, *, seqs, rows):
    @pl.when(pl.program_id(1) == 0)
    def _():
        carry_ref[...] = jnp.zeros_like(carry_ref)
        srw_ref[...] = jnp.zeros_like(srw_ref)
        sret_ref[...] = jnp.zeros_like(sret_ref)

    n_chunks = rows // CHUNK
    w = RWKV_WIDTH
    vecs = vecs_ref[...]
    w0, a0, k_k, k_a, r_k, ln_w, ln_b = (vecs[i:i + 1] for i in range(7))
    intra_decay = [dd_ref[:, p * RET_QK_PAIR:(p + 1) * RET_QK_PAIR] for p in range(RET_PAIRS)]
    chunk_decay = [g64_ref[:, p * RET_V_PAIR:(p + 1) * RET_V_PAIR] for p in range(RET_PAIRS)]

    def pieces(z):
        return [z[c * CHUNK:(c + 1) * CHUNK, qd * QUAD:(qd + 1) * QUAD]
                for c in range(n_chunks) for qd in range(N_QUADS)]

    def pairs(z, lanes):
        return [z[c * CHUNK:(c + 1) * CHUNK, p * lanes:(p + 1) * lanes]
                for c in range(n_chunks) for p in range(RET_PAIRS)]

    def front_stages(sq, p_blocks):
        t, f = {}, {}
        q0 = RWKV_IN

        def shift():
            p = t["p"] = jnp.concatenate(p_blocks, axis=1)
            pr = p[:, :RWKV_IN]
            first = lax.broadcasted_iota(jnp.int32, pr.shape, 0) == 0
            prev = jnp.where(first, carry_ref[sq], pltpu.roll(pr, 1, 0))
            carry_ref[sq] = pr[rows - 1:rows, :]
            t["ps"] = pr + mu_ref[...] * (prev - pr)

        def loras():
            ps = t["ps"]
            wa = ps[:, 3 * w:3 * w + DECAY_LORA + ICL_LORA]
            gd = ps[:, 3 * w + DECAY_LORA + ICL_LORA:]
            is_decay = lax.broadcasted_iota(jnp.int32, wa.shape, 1) < DECAY_LORA
            lora = _dot(jnp.where(is_decay, jnp.tanh(wa), wa).astype(BF16), lora_ref[...])
            log_w = -_softplus(-(w0 + lora[:, :w])) - 0.5
            t["log_decay"] = -jnp.exp(log_w)
            t["a"] = _sigmoid(a0 + lora[:, w:])
            f["g"] = _dot(_sigmoid(gd).astype(BF16), glora_ref[...])

        def keys():
            ps = t["ps"]
            r, k, v = ps[:, :w], ps[:, w:2 * w], ps[:, 2 * w:3 * w]
            kk = k * k_k
            t["kk"] = kk * lax.rsqrt(jnp.maximum(_head_sum(kk * kk), 1e-24))
            t["k_mod"] = k * (1.0 + (t["a"] - 1.0) * k_a)
            f["bonus"] = _head_sum(r * t["k_mod"] * r_k) * v
            f["v"] = pieces(v.astype(BF16))

        def decays():
            cum = _const_mm(tri_ref[...], t["log_decay"], 2)
            tot = jnp.concatenate(
                [jnp.broadcast_to(cum[(i + 1) * CHUNK - 1:(i + 1) * CHUNK], (CHUNK, w))
                 for i in range(n_chunks)], axis=0)
            t["cum"], t["tot"] = cum, tot
            tot_t = [jnp.exp(tot[:, qd * QUAD:(qd + 1) * QUAD]).T for qd in range(N_QUADS)]
            f["w_tot"] = [tot_t[qd][:, c * CHUNK:c * CHUNK + 1]
                          for c in range(n_chunks) for qd in range(N_QUADS)]
            f["at"] = pieces((-t["kk"] * jnp.exp(cum - t["log_decay"])).astype(BF16))
            f["rt"] = pieces((t["ps"][:, :w] * jnp.exp(cum)).astype(BF16))

        def scaled():
            kka = t["kk"] * t["a"]
            w_inv = jnp.exp(-t["cum"])
            w_end = jnp.exp(t["tot"] - t["cum"])
            scale = lambda z, s: pieces((z * s).astype(BF16))
            f["bw"], f["kw"] = scale(kka, w_end), scale(t["k_mod"], w_end)

            def rhs(z):
                per_quad = [_chunk_rhs(z[:, qd * QUAD:(qd + 1) * QUAD], GROUP)
                            for qd in range(N_QUADS)]
                return [per_quad[qd][c] for c in range(n_chunks) for qd in range(N_QUADS)]
            f["wb"], f["wk"] = rhs(kka * w_inv), rhs(t["k_mod"] * w_inv)

        def retention():
            p = t["p"]
            cos = cos_ref[...]
            sin = sin_ref[...]
            rq = _rope(p[:, q0:q0 + RET_QK_WIDTH], cos, sin)
            rk = _rope(p[:, q0 + RET_QK_WIDTH:q0 + 2 * RET_QK_WIDTH], cos, sin) \
                * (RET_QK_DIM ** -0.5)
            qk = lambda z: pairs(z.astype(BF16), RET_QK_PAIR)
            f["rq"] = qk(rq)
            per_pair = [_chunk_rhs(rk[:, p_ * RET_QK_PAIR:(p_ + 1) * RET_QK_PAIR], 2)
                        for p_ in range(RET_PAIRS)]
            f["rk"] = [per_pair[p_][c] for c in range(n_chunks) for p_ in range(RET_PAIRS)]
            f["rqw"], f["rkw"] = qk(rq * qw_ref[...]), qk(rk * kw_ref[...])
            f["rv"] = pairs(p[:, q0 + 2 * RET_QK_WIDTH:q0 + 2 * RET_QK_WIDTH + RET_V_WIDTH]
                            .astype(BF16), RET_V_PAIR)
            f["rg"] = p[:, q0 + 2 * RET_QK_WIDTH + RET_V_WIDTH:]

        return [shift, loras, keys, decays, scaled, retention], f

    def local(f, fill):
        a_hat, u_loc, a_rb, a_rk_v = _rwkv_local(f["at"], f["rt"], f["wb"], f["wk"], f["v"], fill)
        intra, kv = _ret_local(f["rq"], f["rk"], f["rkw"], f["rv"], intra_decay * n_chunks, fill)
        return dict(a_hat=a_hat, u_loc=u_loc, a_rb=a_rb, a_rk_v=a_rk_v, intra=intra, kv=kv)

    def state_stages(sq, f, loc):
        st = dict(rw=[srw_ref[sq * N_QUADS + qd] for qd in range(N_QUADS)],
                  ret=[sret_ref[sq * RET_PAIRS + p] for p in range(RET_PAIRS)])
        res = dict(y=[None] * n_chunks, o=[None] * n_chunks)
        tmp = {}

        def quads(name, c):
            return f[name][c * N_QUADS:(c + 1) * N_QUADS] if name in f else \
                loc[name][c * N_QUADS:(c + 1) * N_QUADS]

        def read_state(c):
            tmp["ps"] = _each(lambda a_, r_, s_: _mm(
                jnp.concatenate([a_.astype(BF16), r_], axis=0), s_),
                              quads("a_hat", c), quads("rt", c), st["rw"])
            ps_ = slice(c * RET_PAIRS, (c + 1) * RET_PAIRS)
            res["o"][c] = jnp.concatenate(
                _each(lambda i_, q_, s_: i_ + _mm(q_, s_), loc["intra"][ps_], f["rqw"][ps_],
                      st["ret"]), axis=1)
            st["ret"] = _each(lambda s_, d_, kv_: s_ * d_ + kv_, st["ret"], chunk_decay,
                              loc["kv"][ps_])

        def update_state(c):
            tmp["u"] = _each(lambda p_, u_: p_[:CHUNK] + u_, tmp["ps"], quads("u_loc", c))
            upd = _each(lambda u_, v_, bw_, kw_: _mm(jnp.concatenate([bw_, kw_], axis=0),
                                                     jnp.concatenate([u_.astype(BF16), v_], axis=0),
                                                     TN),
                        tmp["u"], quads("v", c), quads("bw", c), quads("kw", c))
            st["rw"] = _each(lambda s_, w_, d_: s_ * w_ + _mask_bd(d_, CHUNK),
                             st["rw"], quads("w_tot", c), upd)
            if c == n_chunks - 1:
                for qd in range(N_QUADS):
                    srw_ref[sq * N_QUADS + qd] = st["rw"][qd]
                for p in range(RET_PAIRS):
                    sret_ref[sq * RET_PAIRS + p] = st["ret"][p]

        def emit_y(c, ps, u):
            y = _each(lambda p_, a_, u_, b_: p_[CHUNK:] + b_ + _mm(a_, _bd_rows(u_, CHUNK)),
                      ps, quads("a_rb", c), u, quads("a_rk_v", c))
            res["y"][c] = jnp.concatenate(y, axis=1)

        stages = []
        for c in range(n_chunks):
            stages.append(functools.partial(read_state, c))
            stages.append(functools.partial(update_state, c))
            stages.append(lambda c=c: emit_y(c, tmp["ps"], tmp["u"]))
        return stages, res

    def output_stages(sq, x, f, res):
        tmp = {}
        blocks = []

        def gates():
            y = jnp.concatenate(res["y"], axis=0)
            inv_n = 1.0 / RWKV_HEAD_DIM
            yc = y - _head_sum(y) * inv_n
            var = _head_sum(yc * yc) * inv_n
            y = (yc * lax.rsqrt(var + LN_X_EPS) * ln_w + ln_b + f["bonus"]) * f["g"]
            o = jnp.concatenate(res["o"], axis=0)
            o = jnp.concatenate(
                [o[:, i * RET_V_DIM:(i + 1) * RET_V_DIM]
                 * lax.rsqrt(jnp.mean(jnp.square(o[:, i * RET_V_DIM:(i + 1) * RET_V_DIM]),
                                      axis=-1, keepdims=True) + NORM_EPS)
                 for i in range(RET_HEADS)], axis=1)
            rg = f["rg"]
            o = o * (rg * _sigmoid(rg))
            tmp["mixed"] = jnp.concatenate([y, o], axis=1).astype(BF16)

        def block(j):
            cs = slice(j * MXU_TILE, (j + 1) * MXU_TILE)
            blocks.append(x[:, cs] + _dot(tmp["mixed"], wout_ref[:, cs]))
            if j == D_MODEL // MXU_TILE - 1:
                o_ref[sq] = jnp.concatenate(blocks, axis=1)

        return [gates] + [functools.partial(block, j) for j in range(D_MODEL // MXU_TILE)]

    def proj_stages(h, out):
        def block(j):
            out.append(_dot(h, win_ref[:, j * MXU_TILE:(j + 1) * MXU_TILE]))
        return [functools.partial(block, j) for j in range(PROJ_WIDTH // MXU_TILE)]

    xs = [x_ref[sq] for sq in range(seqs)]
    hs = [_rms_norm(x, gain_ref[...]).astype(BF16) for x in xs]
    def interleave(a, b):
        out = []
        for i in range(max(len(a), len(b))):
            out += a[i:i + 1] + b[i:i + 1]
        return out

    p_blocks = [[] for _ in range(seqs)]
    fronts = [front_stages(sq, p_blocks[sq]) for sq in range(seqs)]
    _Fill(proj_stages(hs[0], p_blocks[0])).drain()
    _Fill(interleave(fronts[0][0], proj_stages(hs[1], p_blocks[1]) if seqs > 1 else [])).drain()
    state_todo, output_todo = [], []
    for sq in range(seqs):
        f = fronts[sq][1]
        ahead = fronts[sq + 1][0] if sq + 1 < seqs else []
        ahead = interleave(ahead, proj_stages(hs[sq + 2], p_blocks[sq + 2]) if sq + 2 < seqs else [])
        fill = _Fill(interleave(interleave(state_todo, output_todo), ahead), LOCAL_STAGES)
        loc = local(f, fill)
        fill.drain()
        stages, res = state_stages(sq, f, loc)
        output_todo = output_stages(sq - 1, xs[sq - 1], *done) if sq else []
        state_todo, done = stages, (f, res)
    _Fill(interleave(state_todo, output_todo)).drain()
    _Fill(output_stages(seqs - 1, xs[seqs - 1], *done)).drain()


def _ffn_kernel(*refs, final_norm, n_casts):
    x_ref, gain_ref, wg_ref, wu_ref, wd_ref, fin_ref = refs[:6]
    o_ref = refs[6 + n_casts]
    for src, dst in zip(refs[6:6 + n_casts], refs[7 + n_casts:]):
        dst[...] = src[...].astype(BF16)
    x = x_ref[...]
    h = _rms_norm(x, gain_ref[...]).astype(BF16)
    gate = _dot(h, wg_ref[...])
    up = _dot(h, wu_ref[...])
    act = (gate * _sigmoid(gate) * up).astype(BF16)
    y = x + 0.5 * _dot(act, wd_ref[...])
    if final_norm:
        y = _rms_norm(y, fin_ref[...])
    o_ref[...] = y


def _resident(shape):
    return pl.BlockSpec(shape, lambda *_: (0,) * len(shape), pipeline_mode=pl.Buffered(1))


def _layer(shape, l):
    return pl.BlockSpec((None,) + shape, lambda *_: (l,) + (0,) * len(shape),
                        pipeline_mode=pl.Buffered(1))


def _weight(w, l):
    return _layer(w.shape[1:], l) if w.ndim == 3 else _resident(w.shape)


def _ffn(x2, l, gain, wg, wu, wd, fin, final_norm, casts=()):
    t = x2.shape[0]
    rows = min(FFN_ROWS, t)
    steps = t // rows
    assert t % rows == 0

    def row_blocks(n_rows):
        return max(n for n in range(1, steps + 1)
                   if steps % n == 0 and n_rows % (n * BF16_ROWS) == 0)

    views = [(w, cl, row_blocks(w.shape[1])) for w, cl in casts]
    out = pl.pallas_call(
        functools.partial(_ffn_kernel, final_norm=final_norm, n_casts=len(views)),
        out_shape=[jax.ShapeDtypeStruct(x2.shape, F32)]
                  + [jax.ShapeDtypeStruct(w.shape[1:], BF16) for w, _, _ in views],
        grid=(steps,),
        in_specs=[pl.BlockSpec((rows, D_MODEL), lambda i: (i, 0)),
                  _layer((1, D_MODEL), l), _weight(wg, l), _weight(wu, l), _weight(wd, l),
                  _resident((1, D_MODEL))]
                 + [pl.BlockSpec((None, w.shape[1] // n, w.shape[2]),
                                 lambda i, cl=cl, n=n: (cl, i * n // steps, 0))
                    for w, cl, n in views],
        out_specs=[pl.BlockSpec((rows, D_MODEL), lambda i: (i, 0))]
                  + [pl.BlockSpec((w.shape[1] // n, w.shape[2]), lambda i, n=n: (i * n // steps, 0))
                     for w, _, n in views],
        compiler_params=pltpu.CompilerParams(
            dimension_semantics=("arbitrary",), vmem_limit_bytes=VMEM_LIMIT_BYTES),
        name="ffn_final" if final_norm else "ffn",
    )(x2, gain, wg, wu, wd, fin, *[w for w, _, _ in views])
    return out[0], list(out[1:])


def _mixer_tables(seq, rows):
    idx = jnp.arange(rows)
    same_chunk = (idx[:, None] // CHUNK) == (idx[None, :] // CHUNK)
    tri = (same_chunk & (idx[None, :] <= idx[:, None])).astype(BF16)
    half = RET_QK_DIM // 2
    inv_freq = 1.0 / (ROPE_BASE ** jnp.linspace(0.0, 1.0, half, dtype=F32))
    ang = jnp.arange(seq, dtype=F32)[:, None] * inv_freq[None, :]
    cos = jnp.tile(jnp.cos(ang), (1, 2 * RET_HEADS))
    sin = jnp.tile(jnp.concatenate([-jnp.sin(ang), jnp.sin(ang)], axis=1), (1, RET_HEADS))
    log_gamma = jnp.log(1.0 - jnp.power(2.0, -5.0 - jnp.arange(RET_HEADS, dtype=F32)))
    pos = jnp.arange(CHUNK, dtype=F32)
    dist = jnp.abs(pos[:, None] - pos[None, :])
    dd = jnp.exp(log_gamma[:, None, None] * dist)
    dd = jnp.transpose(dd, (1, 0, 2)).reshape(CHUNK, RET_HEADS * CHUNK)
    query_w = jnp.exp(log_gamma[:, None] * (pos + 1.0)[None, :])
    key_w = jnp.exp(log_gamma[:, None] * (CHUNK - 1.0 - pos)[None, :])
    qw = jnp.tile(jnp.repeat(query_w.T, RET_QK_DIM, axis=1), (rows // CHUNK, 1))
    kw = jnp.tile(jnp.repeat(key_w.T, RET_QK_DIM, axis=1), (rows // CHUNK, 1))
    g64 = jnp.repeat(jnp.exp(log_gamma * CHUNK), RET_V_DIM)[None, :]
    return tri, cos, sin, dd, qw, kw, g64


def _mixer(x, l, gain, w_in, mu, lora, glora, vecs, w_out, tables):
    b, seq, _ = x.shape
    rows = min(MIX_ROWS, seq)
    seqs = MIX_SEQS if b % MIX_SEQS == 0 else 1
    assert seq % rows == 0 and rows % CHUNK == 0
    tri, cos, sin, dd, qw, kw, g64 = tables
    return pl.pallas_call(
        functools.partial(_mixer_kernel, seqs=seqs, rows=rows),
        out_shape=jax.ShapeDtypeStruct(x.shape, F32),
        grid=(b // seqs, seq // rows),
        in_specs=[pl.BlockSpec((seqs, rows, D_MODEL), lambda i, j: (i, j, 0)),
                  _layer((1, D_MODEL), l), _weight(w_in, l),
                  _layer((1, RWKV_IN), l), _layer((DECAY_LORA + ICL_LORA, 2 * RWKV_WIDTH), l),
                  _layer((GATE_LORA, RWKV_WIDTH), l), _layer((8, RWKV_WIDTH), l),
                  _weight(w_out, l),
                  _resident((rows, rows)),
                  pl.BlockSpec((rows, RET_QK_WIDTH), lambda i, j: (j, 0)),
                  pl.BlockSpec((rows, RET_QK_WIDTH), lambda i, j: (j, 0)),
                  _resident((CHUNK, RET_QK_WIDTH)), _resident((rows, RET_QK_WIDTH)),
                  _resident((rows, RET_QK_WIDTH)), _resident((1, RET_V_WIDTH))],
        out_specs=pl.BlockSpec((seqs, rows, D_MODEL), lambda i, j: (i, j, 0)),
        scratch_shapes=[pltpu.VMEM((seqs, 1, RWKV_IN), F32),
                        pltpu.VMEM((seqs * N_QUADS, QUAD, QUAD), F32),
                        pltpu.VMEM((seqs * RET_PAIRS, RET_QK_PAIR, RET_V_PAIR), F32)],
        compiler_params=pltpu.CompilerParams(
            dimension_semantics=("arbitrary", "arbitrary"), vmem_limit_bytes=VMEM_LIMIT_BYTES),
        name="mixer",
    )(x, gain, w_in, mu, lora, glora, vecs, w_out, tri, cos, sin, dd, qw, kw, g64)


def kernel(x, ffn1_norm, ffn1_w_gate, ffn1_w_up, ffn1_w_down, mix_norm, w_in, shift_mu, w0, w_lora_up, a0, a_lora_up, g_lora_up, k_k, k_a, r_k, ln_x_w, ln_x_b, w_out, ffn2_norm, ffn2_w_gate, ffn2_w_up, ffn2_w_down, final_norm):
    b, seq, d = x.shape
    depth = w_in.shape[0]
    tables = _mixer_tables(seq, min(MIX_ROWS, seq))
    bf = lambda z: z.astype(BF16)
    row = lambda z: z.reshape(depth, 1, -1)
    ffn1_w, ffn2_w = (ffn1_w_gate, ffn1_w_up, ffn1_w_down), (ffn2_w_gate, ffn2_w_up, ffn2_w_down)
    fin = final_norm[None, :]
    zeros = jnp.zeros((depth, DECAY_LORA, RWKV_WIDTH), F32)
    lora = bf(jnp.concatenate(
        [jnp.concatenate([w_lora_up, zeros], axis=2),
         jnp.concatenate([zeros, a_lora_up], axis=2)], axis=1))
    vecs = jnp.stack([w0, a0, k_k, k_a, r_k.reshape(depth, -1), ln_x_w, ln_x_b,
                      jnp.zeros((depth, RWKV_WIDTH), F32)], axis=1)
    glora = bf(g_lora_up)
    w1 = [bf(w[0]) for w in ffn1_w]
    for l in range(depth):
        nxt = l + 1 < depth
        x, cast = _ffn(x.reshape(b * seq, d), l, row(ffn1_norm), *w1, fin, False,
                       casts=[(w, l) for w in ffn2_w + (w_in, w_out)])
        w2, wm = cast[:3], cast[3:]
        x = _mixer(x.reshape(b, seq, d), l, row(mix_norm), wm[0], row(shift_mu), lora, glora,
                   vecs, wm[1], tables)
        x, w1 = _ffn(x.reshape(b * seq, d), l, row(ffn2_norm), *w2, fin, not nxt,
                     casts=[(w, l + 1) for w in ffn1_w] if nxt else [])
    return x.reshape(b, seq, d)
```
